```python
import math
import jax, jax.numpy as jnp
from jax import lax
import numpy as np

D_MODEL = 1024
BATCH = 8
SEQ = 2048
DEPTH = 2
DEC_BATCH = 32
DEC_SEQ = 1
PAST_LEN = 8192
PAGE_SIZE = 128

N_A_LAYERS = DEPTH // 2
N_B_LAYERS = DEPTH - N_A_LAYERS
SSM_EXPAND = 2
D_INNER = SSM_EXPAND * D_MODEL
SSM_HEAD_DIM = 64
SSM_HEADS = D_INNER // SSM_HEAD_DIM
N_GROUPS = 4
HEADS_PER_GROUP = SSM_HEADS // N_GROUPS
D_STATE = 128
D_CONV = 4
CONV_DIM = D_INNER + 2 * N_GROUPS * D_STATE
IN_A_DIM = D_INNER + CONV_DIM + SSM_HEADS
CHUNK = 128
ATT_HEADS = D_MODEL // 128
HEAD_DIM = 64
V_DIM = 2 * HEAD_DIM
K_DIM = ATT_HEADS * 2 * HEAD_DIM
VAL_WIDTH = ATT_HEADS * V_DIM
IN_B_DIM = K_DIM + VAL_WIDTH
KV_DIM = K_DIM + VAL_WIDTH
ROT_DIM = HEAD_DIM // 4
ROPE_THETA = 500000.0
Q_BLOCK = 128
EPS = 1e-6
NEG_INF = -1e30

kernel_name = 'hybrid_ssd_diffattn_yoco_step'


def _rmsnorm(x, g):
    xf = x.astype(jnp.float32)
    xf = xf * lax.rsqrt(jnp.mean(xf * xf, axis=-1, keepdims=True) + EPS)
    return xf.astype(x.dtype) * g


def _pad_time(t, total):
    pad = total - t.shape[1]
    return jnp.pad(t, [(0, 0), (0, pad)] + [(0, 0)] * (t.ndim - 2))


def _ssd(x, dt, a, bm, cm, s0):
    b, L = x.shape[:2]
    q = min(CHUNK, L)
    nc = -(-L // q)
    lp = nc * q
    x, dt, bm, cm = (_pad_time(t, lp) for t in (x, dt, bm, cm))
    x = x.reshape(b, nc, q, N_GROUPS, HEADS_PER_GROUP, SSM_HEAD_DIM)
    dt = dt.reshape(b, nc, q, N_GROUPS, HEADS_PER_GROUP)
    bm = bm.reshape(b, nc, q, N_GROUPS, D_STATE).astype(jnp.float32)
    cm = cm.reshape(b, nc, q, N_GROUPS, D_STATE).astype(jnp.float32)
    a_cs = jnp.cumsum(dt * a, axis=2)
    xdt = x.astype(jnp.float32) * dt[..., None]
    seg = a_cs[:, :, :, None] - a_cs[:, :, None, :]
    causal = jnp.tril(jnp.ones((q, q), dtype=bool))[None, None, :, :, None, None]
    decay_ls = jnp.exp(jnp.where(causal, seg, -jnp.inf))
    cb = jnp.einsum('bclgn,bcsgn->bclsg', cm, bm)
    y_diag = jnp.einsum('bclsg,bclsgr,bcsgrp->bclgrp', cb, decay_ls, xdt)
    decay_s = jnp.exp(a_cs[:, :, -1:] - a_cs)
    states = jnp.einsum('bcsgn,bcsgr,bcsgrp->bcgrpn', bm, decay_s, xdt)
    chunk_decay = jnp.exp(a_cs[:, :, -1])

    def step(s, inp):
        st, dec = inp
        return s * dec[..., None, None] + st, s

    s_final, s_in = lax.scan(step, s0, (jnp.moveaxis(states, 1, 0), jnp.moveaxis(chunk_decay, 1, 0)))
    s_in = jnp.moveaxis(s_in, 0, 1)
    y_off = jnp.einsum('bclgn,bcgrpn,bclgr->bclgrp', cm, s_in, jnp.exp(a_cs))
    y = (y_diag + y_off).reshape(b, lp, N_GROUPS, HEADS_PER_GROUP, SSM_HEAD_DIM)[:, :L]
    return y, s_final


def _ssd_layer(x, conv_init, ssm_init, norm_g, w_in, conv_w, conv_b, dt_bias, a_log, d_skip, gnorm, w_out):
    b, L, _ = x.shape
    proj = _rmsnorm(x, norm_g) @ w_in
    z = proj[..., :D_INNER]
    xbc = proj[..., D_INNER:D_INNER + CONV_DIM]
    dt_raw = proj[..., D_INNER + CONV_DIM:]
    xpad = jnp.concatenate([conv_init.astype(xbc.dtype), xbc], axis=1)
    conv = conv_b + sum(xpad[:, k:k + L] * conv_w[k] for k in range(D_CONV))
    new_conv = xpad[:, L:]
    xbc = jax.nn.silu(conv)
    xs = xbc[..., :D_INNER].reshape(b, L, N_GROUPS, HEADS_PER_GROUP, SSM_HEAD_DIM)
    bm = xbc[..., D_INNER:D_INNER + N_GROUPS * D_STATE].reshape(b, L, N_GROUPS, D_STATE)
    cm = xbc[..., D_INNER + N_GROUPS * D_STATE:].reshape(b, L, N_GROUPS, D_STATE)
    dt = jax.nn.softplus((dt_raw + dt_bias).astype(jnp.float32)).reshape(b, L, N_GROUPS, HEADS_PER_GROUP)
    a = -jnp.exp(a_log.astype(jnp.float32)).reshape(N_GROUPS, HEADS_PER_GROUP)
    s0 = ssm_init.astype(jnp.float32).reshape(b, N_GROUPS, HEADS_PER_GROUP, SSM_HEAD_DIM, D_STATE)
    y, s_final = _ssd(xs, dt, a, bm, cm, s0)
    y = y.astype(x.dtype) + d_skip.reshape(N_GROUPS, HEADS_PER_GROUP)[..., None] * xs
    yg = (y.reshape(b, L, D_INNER) * jax.nn.silu(z)).reshape(b, L, N_GROUPS, D_INNER // N_GROUPS)
    yg = _rmsnorm(yg, gnorm.reshape(N_GROUPS, D_INNER // N_GROUPS)).reshape(b, L, D_INNER)
    out = x + yg @ w_out
    return out, new_conv, s_final.reshape(b, SSM_HEADS, SSM_HEAD_DIM, D_STATE)


def _rope(t, pos):
    inv_freq = ROPE_THETA ** (-jnp.arange(0, ROT_DIM, 2, dtype=jnp.float32) / ROT_DIM)
    ang = pos[:, None] * inv_freq[None, :]
    cos = jnp.cos(ang)[None, :, None, None, :].astype(t.dtype)
    sin = jnp.sin(ang)[None, :, None, None, :].astype(t.dtype)
    half = ROT_DIM // 2
    x1, x2 = t[..., :half], t[..., half:ROT_DIM]
    return jnp.concatenate([x1 * cos - x2 * sin, x2 * cos + x1 * sin, t[..., ROT_DIM:]], axis=-1)


def _shared_kv(h, pos, norm_kv, w_kv):
    b, L, _ = h.shape
    kv = _rmsnorm(h, norm_kv) @ w_kv
    k = _rope(kv[..., :K_DIM].reshape(b, L, ATT_HEADS, 2, HEAD_DIM), pos)
    v = kv[..., K_DIM:].reshape(b, L, ATT_HEADS, V_DIM)
    return k, v


def _diff_attention(q, k, v, q_pos, k_pos, lam):
    b, L = q.shape[:2]
    qb = min(Q_BLOCK, L)
    nb = -(-L // qb)
    q = _pad_time(q, nb * qb)
    q_pos = jnp.pad(q_pos, (0, nb * qb - L), mode='edge')
    q_blocks = jnp.moveaxis(q.reshape(b, nb, qb, ATT_HEADS, 2, HEAD_DIM), 1, 0)
    pos_blocks = q_pos.reshape(nb, qb)
    scale = HEAD_DIM ** -0.5

    def block(args):
        qi, pi = args
        s = jnp.einsum('bqhjd,bkhjd->bhjqk', qi, k).astype(jnp.float32) * scale
        s = jnp.where(k_pos[None, :] <= pi[:, None], s, NEG_INF)
        p = jax.nn.softmax(s, axis=-1)
        w = p[:, :, 0] - lam * p[:, :, 1]
        return jnp.einsum('bhqk,bkhe->bqhe', w.astype(v.dtype), v)

    o = lax.map(block, (q_blocks, pos_blocks))
    return jnp.moveaxis(o, 0, 1).reshape(b, nb * qb, ATT_HEADS, V_DIM)[:, :L]


def _diff_layer(x, k, v, q_pos, k_pos, lambda_init, norm_g, w_in, lq1, lk1, lq2, lk2, subln, w_out):
    b, L, _ = x.shape
    proj = _rmsnorm(x, norm_g) @ w_in
    q = _rope(proj[..., :K_DIM].reshape(b, L, ATT_HEADS, 2, HEAD_DIM), q_pos)
    gate = proj[..., K_DIM:]
    lam = (jnp.exp(jnp.sum(lq1.astype(jnp.float32) * lk1.astype(jnp.float32)))
           - jnp.exp(jnp.sum(lq2.astype(jnp.float32) * lk2.astype(jnp.float32))) + lambda_init)
    o = _diff_attention(q, k, v, q_pos, k_pos, lam)
    o = _rmsnorm(o, subln) * (1.0 - lambda_init)
    o = o.reshape(b, L, VAL_WIDTH) * jax.nn.silu(gate)
    return x + o @ w_out


def setup_inputs(seed: int = 0) -> dict:
    key = jax.random.key(seed)
    ks = iter(jax.random.split(key, 40))

    def nrm(shape, scale):
        return jax.random.normal(next(ks), shape, jnp.float32) * scale

    def gain(shape):
        return 1.0 + nrm(shape, 0.02)

    n_pages = PAST_LEN // PAGE_SIZE
    n_used = DEC_BATCH * n_pages
    n_pool = n_used + n_used // 4
    page_table = jax.random.permutation(next(ks), n_pool)[:n_used].astype(jnp.int32).reshape(DEC_BATCH, n_pages)
    dt0 = jnp.exp(jax.random.uniform(next(ks), (N_A_LAYERS, SSM_HEADS), jnp.float32, math.log(1e-3), math.log(1e-1)))
    dt_bias = dt0 + jnp.log(-jnp.expm1(-dt0))
    a_log = jnp.log(jax.random.uniform(next(ks), (N_A_LAYERS, SSM_HEADS), jnp.float32, 1.0, 16.0))
    return {
        'x_prompt': nrm((BATCH, SEQ, D_MODEL), 1.0),
        'x_sample': nrm((DEC_BATCH, DEC_SEQ, D_MODEL), 1.0),
        'cache_k': nrm((n_pool, PAGE_SIZE, ATT_HEADS, 2, HEAD_DIM), 1.0),
        'cache_v': nrm((n_pool, PAGE_SIZE, ATT_HEADS, V_DIM), 1.0),
        'page_table': page_table,
        'state_conv': nrm((N_A_LAYERS, DEC_BATCH, D_CONV - 1, CONV_DIM), 1.0),
        'state_ssm': nrm((N_A_LAYERS, DEC_BATCH, SSM_HEADS, SSM_HEAD_DIM, D_STATE), 0.1),
        'norm_a': gain((N_A_LAYERS, D_MODEL)),
        'w_in_a': nrm((N_A_LAYERS, D_MODEL, IN_A_DIM), D_MODEL ** -0.5),
        'conv_w': nrm((N_A_LAYERS, D_CONV, CONV_DIM), D_CONV ** -0.5),
        'conv_b': nrm((N_A_LAYERS, CONV_DIM), 0.02),
        'dt_bias': dt_bias,
        'a_log': a_log,
        'd_skip': gain((N_A_LAYERS, SSM_HEADS)),
        'gnorm_a': gain((N_A_LAYERS, D_INNER)),
        'w_out_a': nrm((N_A_LAYERS, D_INNER, D_MODEL), D_INNER ** -0.5),
        'norm_kv': gain((D_MODEL,)),
        'w_kv': nrm((D_MODEL, KV_DIM), D_MODEL ** -0.5),
        'norm_b': gain((N_B_LAYERS, D_MODEL)),
        'w_in_b': nrm((N_B_LAYERS, D_MODEL, IN_B_DIM), D_MODEL ** -0.5),
        'lambda_q1': nrm((N_B_LAYERS, HEAD_DIM), 0.1),
        'lambda_k1': nrm((N_B_LAYERS, HEAD_DIM), 0.1),
        'lambda_q2': nrm((N_B_LAYERS, HEAD_DIM), 0.1),
        'lambda_k2': nrm((N_B_LAYERS, HEAD_DIM), 0.1),
        'subln_b': gain((N_B_LAYERS, V_DIM)),
        'w_out_b': nrm((N_B_LAYERS, VAL_WIDTH, D_MODEL), VAL_WIDTH ** -0.5),
        'norm_f': gain((D_MODEL,)),
    }


def reference(x_prompt, x_sample, cache_k, cache_v, page_table, state_conv, state_ssm,
              norm_a, w_in_a, conv_w, conv_b, dt_bias, a_log, d_skip, gnorm_a, w_out_a,
              norm_kv, w_kv, norm_b, w_in_b, lambda_q1, lambda_k1, lambda_q2, lambda_k2,
              subln_b, w_out_b, norm_f):
    n_pages = PAST_LEN // PAGE_SIZE
    pos_p = jnp.arange(SEQ, dtype=jnp.float32)
    pos_s = PAST_LEN + jnp.arange(DEC_SEQ, dtype=jnp.float32)
    kpos_s = jnp.arange(PAST_LEN + DEC_SEQ, dtype=jnp.float32)
    past_k = cache_k[page_table].reshape(DEC_BATCH, n_pages * PAGE_SIZE, ATT_HEADS, 2, HEAD_DIM)
    past_v = cache_v[page_table].reshape(DEC_BATCH, n_pages * PAGE_SIZE, ATT_HEADS, V_DIM)
    conv0_p = jnp.zeros((BATCH, D_CONV - 1, CONV_DIM), x_prompt.dtype)
    ssm0_p = jnp.zeros((BATCH, SSM_HEADS, SSM_HEAD_DIM, D_STATE), jnp.float32)

    hp, hs = x_prompt, x_sample
    conv_p, ssm_p, conv_s, ssm_s = [], [], [], []
    k_p = v_p = k_s = v_s = k_all = v_all = None
    for layer in range(DEPTH):
        if layer < N_A_LAYERS:
            i = layer
            wa = (norm_a[i], w_in_a[i], conv_w[i], conv_b[i], dt_bias[i], a_log[i], d_skip[i], gnorm_a[i], w_out_a[i])
            hp, c, s = _ssd_layer(hp, conv0_p, ssm0_p, *wa)
            conv_p.append(c)
            ssm_p.append(s)
            hs, c, s = _ssd_layer(hs, state_conv[i], state_ssm[i], *wa)
            conv_s.append(c)
            ssm_s.append(s)
        else:
            j = layer - N_A_LAYERS
            if j == 0:
                k_p, v_p = _shared_kv(hp, pos_p, norm_kv, w_kv)
                k_s, v_s = _shared_kv(hs, pos_s, norm_kv, w_kv)
                k_all = jnp.concatenate([past_k.astype(k_s.dtype), k_s], axis=1)
                v_all = jnp.concatenate([past_v.astype(v_s.dtype), v_s], axis=1)
            lambda_init = 0.8 - 0.6 * math.exp(-0.3 * layer)
            wb = (norm_b[j], w_in_b[j], lambda_q1[j], lambda_k1[j], lambda_q2[j], lambda_k2[j], subln_b[j], w_out_b[j])
            hp = _diff_layer(hp, k_p, v_p, pos_p, pos_p, lambda_init, *wb)
            hs = _diff_layer(hs, k_all, v_all, pos_s, kpos_s, lambda_init, *wb)

    y_prompt = _rmsnorm(hp, norm_f)
    y_sample = _rmsnorm(hs, norm_f)
    return (y_prompt, y_sample, k_p, v_p, jnp.stack(conv_p), jnp.stack(ssm_p),
            k_s, v_s, jnp.stack(conv_s), jnp.stack(ssm_s))
```

```python
import functools
import math

import jax
import jax.numpy as jnp
from jax import lax
from jax.experimental import pallas as pl
from jax.experimental.pallas import tpu as pltpu

F32 = jnp.float32
BF16 = jnp.bfloat16
HIGHEST = lax.Precision.HIGHEST

D_MODEL = 1024
PAST_LEN = 8192
PAGE_SIZE = 128
N_PAGES = PAST_LEN // PAGE_SIZE
D_INNER = 2048
SSM_HEAD_DIM = 64
SSM_HEADS = 32
N_GROUPS = 4
D_STATE = 128
D_CONV = 4
CONV_DIM = D_INNER + 2 * N_GROUPS * D_STATE
CHUNK = 128
ATT_HEADS = 8
HEAD_DIM = 64
V_DIM = 128
K_DIM = 1024
VAL_WIDTH = 1024
ROT_DIM = 16
ROPE_THETA = 500000.0
EPS = 1e-6
NEG_INF = -1e30
LAMBDA_INIT = 0.8 - 0.6 * math.exp(-0.3 * 1)

LANES = 128
VMEM_LIMIT = 56 * 1024 * 1024

NT_DIMS = (((1,), (1,)), ((), ()))
TN_DIMS = (((0,), (0,)), ((), ()))


def _params(*sem):
    return pltpu.CompilerParams(dimension_semantics=sem, vmem_limit_bytes=VMEM_LIMIT)


def _silu(x):
    return x * (1.0 / (1.0 + jnp.exp(-x)))


def _softplus(x):
    return jnp.maximum(x, 0.0) + jnp.log1p(jnp.exp(-jnp.abs(x)))


def _rms(x):
    return x * lax.rsqrt(jnp.mean(x * x, axis=-1, keepdims=True) + EPS)


def _dot(a, b):
    return jnp.dot(a, b, preferred_element_type=F32)


def _full(shape):
    return pl.BlockSpec(shape, lambda *_: (0,) * len(shape))


def _inproj_a_kernel(x_ref, g_ref, wz_ref, wx_ref, wdt_ref, z_ref, xbc_ref, dt_ref):
    xn = (_rms(x_ref[...]) * g_ref[...]).astype(BF16)
    z_ref[...] = _dot(xn, wz_ref[...])
    xbc_ref[...] = _dot(xn, wx_ref[...])
    dt_ref[...] = _dot(xn, wdt_ref[...])


def _inproj_a(x, g, wz, wx, wdt, tm):
    m = x.shape[0]
    row = lambda n: pl.BlockSpec((tm, n), lambda i: (i, 0))
    return pl.pallas_call(
        _inproj_a_kernel,
        grid=(m // tm,),
        in_specs=[row(D_MODEL), _full((1, D_MODEL)), _full(wz.shape), _full(wx.shape), _full(wdt.shape)],
        out_specs=[row(D_INNER), row(CONV_DIM), row(LANES)],
        out_shape=[jax.ShapeDtypeStruct((m, D_INNER), F32),
                   jax.ShapeDtypeStruct((m, CONV_DIM), F32),
                   jax.ShapeDtypeStruct((m, LANES), F32)],
        compiler_params=_params("arbitrary"),
        name="inproj_a",
    )(x, g, wz, wx, wdt)


def _gate_norm_outproj(y, z, gnorm, wout, xres):
    yg = y * _silu(z)
    gw = D_INNER // N_GROUPS
    parts = [_rms(yg[:, g * gw:(g + 1) * gw]) for g in range(N_GROUPS)]
    yn = (jnp.concatenate(parts, axis=1) * gnorm).astype(BF16)
    return xres + _dot(yn, wout)


def _ssd_prompt_kernel(xbc_ref, dt_ref, z_ref, x_ref, convw_ref, convb_ref, dtb_ref, alog_ref,
                       dskip_ref, gnorm_ref, wout_ref,
                       h_ref, conv_out_ref, ssm_out_ref,
                       xpad_ref, act_ref, y_ref, st_ref):
    c = pl.program_id(1)
    last = pl.num_programs(1) - 1
    q = CHUNK

    @pl.when(c == 0)
    def _():
        xpad_ref[0:8, :] = jnp.zeros((8, CONV_DIM), F32)
        st_ref[...] = jnp.zeros_like(st_ref)

    xbc = xbc_ref[0]
    xpad_ref[8:8 + q, :] = xbc
    conv = convb_ref[...] + xbc * convw_ref[3:4, :]
    for k in range(D_CONV - 1):
        conv = conv + xpad_ref[5 + k:5 + k + q, :] * convw_ref[k:k + 1, :]
    act_ref[...] = _silu(conv)
    xpad_ref[0:8, :] = xpad_ref[q:q + 8, :]

    @pl.when(c == last)
    def _():
        conv_out_ref[0] = xpad_ref[5:8, :]

    dt = _softplus(dt_ref[0] + dtb_ref[...])
    a = -jnp.exp(alog_ref[...])
    row_i = lax.broadcasted_iota(jnp.int32, (q, q), 0)
    col_i = lax.broadcasted_iota(jnp.int32, (q, q), 1)
    causal = row_i >= col_i
    a_cs = jnp.dot(causal.astype(F32), dt * a, precision=HIGHEST, preferred_element_type=F32)
    a_cs_t = a_cs.T
    dt_t = dt.T
    w_t = dt_t * jnp.exp(a_cs_t[:, q - 1:q] - a_cs_t)
    e_cs = jnp.exp(a_cs)
    first_half = col_i < SSM_HEAD_DIM

    for g in range(N_GROUPS):
        bm = act_ref[:, D_INNER + g * D_STATE:D_INNER + (g + 1) * D_STATE]
        cm = act_ref[:, D_INNER + (N_GROUPS + g) * D_STATE:D_INNER + (N_GROUPS + g + 1) * D_STATE]
        bm_b = bm.astype(BF16)
        cm_b = cm.astype(BF16)
        cb = lax.dot_general(cm_b, bm_b, NT_DIMS, preferred_element_type=F32)
        bm_t = bm.T
        for jj in range(SSM_HEADS // N_GROUPS // 2):
            j = g * (SSM_HEADS // N_GROUPS // 2) + jj
            xs = act_ref[:, j * LANES:(j + 1) * LANES]
            xs_b = xs.astype(BF16)
            zero = jnp.zeros_like(xs_b)
            xbd = jnp.concatenate([jnp.where(first_half, xs_b, zero),
                                   jnp.where(first_half, zero, xs_b)], axis=0)
            m_parts, w_parts = [], []
            for h in (2 * j, 2 * j + 1):
                seg = a_cs[:, h:h + 1] - a_cs_t[h:h + 1, :]
                decay = jnp.exp(jnp.where(causal, seg, -jnp.inf))
                m_parts.append((cb * decay * dt_t[h:h + 1, :]).astype(BF16))
                w_parts.append((bm_t * w_t[h:h + 1, :]).astype(BF16))
            y_diag = _dot(jnp.concatenate(m_parts, axis=1), xbd)
            upd_t = _dot(jnp.concatenate(w_parts, axis=1), xbd)
            st = st_ref[j]
            e_sel = jnp.where(first_half, e_cs[:, 2 * j:2 * j + 1], e_cs[:, 2 * j + 1:2 * j + 2])
            y_off = _dot(cm_b, st.astype(BF16)) * e_sel
            st_new = st * e_sel[q - 1:q, :] + upd_t
            st_ref[j] = st_new
            y_ref[:, j * LANES:(j + 1) * LANES] = y_diag + y_off + dskip_ref[:, j * LANES:(j + 1) * LANES] * xs

            @pl.when(c == last)
            def _():
                ssm_out_ref[0, 2 * j:2 * j + 2] = st_new.T.reshape(2, SSM_HEAD_DIM, D_STATE)

    h_ref[0] = _gate_norm_outproj(y_ref[...], z_ref[0], gnorm_ref[...], wout_ref[...], x_ref[0])


def _ssd_prompt(xbc, dt, z, x, convw, convb, dtb, alog, dskip, gnorm, wout):
    b, l, _ = x.shape
    nc = l // CHUNK
    tok = lambda n: pl.BlockSpec((1, CHUNK, n), lambda i, c: (i, c, 0))
    return pl.pallas_call(
        _ssd_prompt_kernel,
        grid=(b, nc),
        in_specs=[tok(CONV_DIM), tok(LANES), tok(D_INNER), tok(D_MODEL),
                  _full(convw.shape), _full(convb.shape), _full(dtb.shape), _full(alog.shape),
                  _full(dskip.shape), _full(gnorm.shape), _full(wout.shape)],
        out_specs=[tok(D_MODEL),
                   pl.BlockSpec((1, D_CONV - 1, CONV_DIM), lambda i, c: (i, 0, 0)),
                   pl.BlockSpec((1, SSM_HEADS, SSM_HEAD_DIM, D_STATE), lambda i, c: (i, 0, 0, 0))],
        out_shape=[jax.ShapeDtypeStruct((b, l, D_MODEL), F32),
                   jax.ShapeDtypeStruct((b, D_CONV - 1, CONV_DIM), F32),
                   jax.ShapeDtypeStruct((b, SSM_HEADS, SSM_HEAD_DIM, D_STATE), F32)],
        scratch_shapes=[pltpu.VMEM((CHUNK + 8, CONV_DIM), F32),
                        pltpu.VMEM((CHUNK, CONV_DIM), F32),
                        pltpu.VMEM((CHUNK, D_INNER), F32),
                        pltpu.VMEM((SSM_HEADS // 2, D_STATE, LANES), F32)],
        compiler_params=_params("arbitrary", "arbitrary"),
        name="ssd_prompt",
    )(xbc, dt, z, x, convw, convb, dtb, alog, dskip, gnorm, wout)


def _ssd_step_pre_kernel(xbc_ref, dt_ref, sconv_ref, convw_ref, convb_ref, dtb_ref, alog_ref,
                         hexp_ref, gexp_ref,
                         conv_out_ref, xdt_ref, dec_ref, bm_ref, cm_ref, ydiag_ref, xs_ref):
    xbc = xbc_ref[...]
    conv = convb_ref[...] + xbc * convw_ref[3:4, :]
    for k in range(D_CONV - 1):
        conv = conv + sconv_ref[:, k, :] * convw_ref[k:k + 1, :]
    conv_out_ref[:, 0, :] = sconv_ref[:, 1, :]
    conv_out_ref[:, 1, :] = sconv_ref[:, 2, :]
    conv_out_ref[:, 2, :] = xbc
    act = _silu(conv)
    xs = act[:, :D_INNER]
    bm = act[:, D_INNER:D_INNER + N_GROUPS * D_STATE]
    cm = act[:, D_INNER + N_GROUPS * D_STATE:]
    dt = _softplus(dt_ref[...] + dtb_ref[...])
    dec = jnp.exp(dt * -jnp.exp(alog_ref[...]))
    hexp = hexp_ref[...]
    dt_x = jnp.dot(dt, hexp, precision=HIGHEST, preferred_element_type=F32)
    dec_x = jnp.dot(dec, hexp, precision=HIGHEST, preferred_element_type=F32)
    prod = bm * cm
    cb = jnp.concatenate([jnp.sum(prod[:, g * D_STATE:(g + 1) * D_STATE], axis=1, keepdims=True)
                          for g in range(N_GROUPS)], axis=1)
    cb_x = jnp.dot(cb, gexp_ref[...], precision=HIGHEST, preferred_element_type=F32)
    xdt = xs * dt_x
    xdt_ref[...] = xdt
    dec_ref[...] = dec_x
    bm_ref[...] = bm
    cm_ref[...] = cm
    ydiag_ref[...] = cb_x * xdt
    xs_ref[...] = xs


def _ssd_step_pre(xbc, dt, sconv, convw, convb, dtb, alog, hexp, gexp):
    m = xbc.shape[0]
    gs = N_GROUPS * D_STATE
    return pl.pallas_call(
        _ssd_step_pre_kernel,
        out_shape=[jax.ShapeDtypeStruct((m, D_CONV - 1, CONV_DIM), F32),
                   jax.ShapeDtypeStruct((m, D_INNER), F32),
                   jax.ShapeDtypeStruct((m, D_INNER), F32),
                   jax.ShapeDtypeStruct((m, gs), F32),
                   jax.ShapeDtypeStruct((m, gs), F32),
                   jax.ShapeDtypeStruct((m, D_INNER), F32),
                   jax.ShapeDtypeStruct((m, D_INNER), F32)],
        compiler_params=pltpu.CompilerParams(vmem_limit_bytes=VMEM_LIMIT),
        name="ssd_step_pre",
    )(xbc, dt, sconv, convw, convb, dtb, alog, hexp, gexp)


def _ssd_step_state_kernel(xdt_ref, dec_ref, bm_ref, cm_ref, st_ref, yoff_ref, st_out_ref):
    gw = D_INNER // N_GROUPS
    hg = SSM_HEADS // N_GROUPS
    sub = lax.broadcasted_iota(jnp.int32, (8, gw), 0)
    sub_n = lax.broadcasted_iota(jnp.int32, (8, D_STATE), 0)
    ones_row = jnp.where(sub_n == 0, 1.0, 0.0).astype(F32)
    for g in range(N_GROUPS):
        xdt8 = jnp.where(sub == 0, xdt_ref[0, :, g * gw:(g + 1) * gw], 0.0)
        dec8 = jnp.where(sub == 0, dec_ref[0, :, g * gw:(g + 1) * gw], 0.0)
        bm8 = jnp.where(sub_n == 0, bm_ref[0, :, g * D_STATE:(g + 1) * D_STATE], 0.0)
        cm8 = jnp.broadcast_to(cm_ref[0, :, g * D_STATE:(g + 1) * D_STATE], (8, D_STATE)).astype(BF16)
        st = st_ref[0, g * hg:(g + 1) * hg].reshape(gw, D_STATE)
        y8 = lax.dot_general(cm8, st.astype(BF16), NT_DIMS, preferred_element_type=F32)
        yoff_ref[0, :, g * gw:(g + 1) * gw] = y8[0:1, :]
        dec_col = lax.dot_general(dec8, ones_row, TN_DIMS, precision=HIGHEST, preferred_element_type=F32)
        upd = lax.dot_general(xdt8, bm8, TN_DIMS, precision=HIGHEST, preferred_element_type=F32)
        st_out_ref[0, g * hg:(g + 1) * hg] = (st * dec_col + upd).reshape(hg, SSM_HEAD_DIM, D_STATE)


def _ssd_step_state(xdt, dec, bm, cm, st):
    m = xdt.shape[0]
    gs = N_GROUPS * D_STATE
    row = lambda n: pl.BlockSpec((1, 1, n), lambda i: (i, 0, 0))
    st_spec = pl.BlockSpec((1, SSM_HEADS, SSM_HEAD_DIM, D_STATE), lambda i: (i, 0, 0, 0))
    yoff, st_out = pl.pallas_call(
        _ssd_step_state_kernel,
        grid=(m,),
        in_specs=[row(D_INNER), row(D_INNER), row(gs), row(gs), st_spec],
        out_specs=[row(D_INNER), st_spec],
        out_shape=[jax.ShapeDtypeStruct((m, 1, D_INNER), F32),
                   jax.ShapeDtypeStruct(st.shape, F32)],
        compiler_params=_params("arbitrary"),
        name="ssd_step_state",
    )(xdt.reshape(m, 1, D_INNER), dec.reshape(m, 1, D_INNER), bm.reshape(m, 1, gs), cm.reshape(m, 1, gs), st)
    return yoff.reshape(m, D_INNER), st_out


def _ssd_step_out_kernel(ydiag_ref, yoff_ref, dec_ref, xs_ref, z_ref, x_ref, dskip_ref, gnorm_ref, wout_ref,
                         h_ref):
    y = ydiag_ref[...] + yoff_ref[...] * dec_ref[...] + dskip_ref[...] * xs_ref[...]
    h_ref[...] = _gate_norm_outproj(y, z_ref[...], gnorm_ref[...], wout_ref[...], x_ref[...])


def _ssd_step_out(ydiag, yoff, dec, xs, z, x, dskip, gnorm, wout):
    return pl.pallas_call(
        _ssd_step_out_kernel,
        out_shape=jax.ShapeDtypeStruct(x.shape, F32),
        compiler_params=pltpu.CompilerParams(vmem_limit_bytes=VMEM_LIMIT),
        name="ssd_step_out",
    )(ydiag, yoff, dec, xs, z, x, dskip, gnorm, wout)


def _rope(t, cos, sin_up, sin_dn):
    cols = []
    for c in range(t.shape[1] // LANES):
        tc = t[:, c * LANES:(c + 1) * LANES]
        cols.append(tc * cos + pltpu.roll(tc, ROT_DIM // 2, 1) * sin_up
                    + pltpu.roll(tc, LANES - ROT_DIM // 2, 1) * sin_dn)
    return jnp.concatenate(cols, axis=1)


def _kvq_kernel(h_ref, gkv_ref, gb_ref, wkv_ref, wb_ref, cos_ref, sup_ref, sdn_ref,
                k_ref, v_ref, q_ref, gate_ref):
    xh = _rms(h_ref[...])
    kv = _dot((xh * gkv_ref[...]).astype(BF16), wkv_ref[...])
    qg = _dot((xh * gb_ref[...]).astype(BF16), wb_ref[...])
    cos, sup, sdn = cos_ref[...], sup_ref[...], sdn_ref[...]
    k_ref[...] = _rope(kv[:, :K_DIM], cos, sup, sdn)
    v_ref[...] = kv[:, K_DIM:]
    q_ref[...] = _rope(qg[:, :K_DIM], cos, sup, sdn)
    gate_ref[...] = qg[:, K_DIM:]


def _kvq(h, gkv, gb, wkv, wb, cos, sup, sdn, tm):
    m = h.shape[0]
    row = lambda n: pl.BlockSpec((tm, n), lambda i: (i, 0))
    out = jax.ShapeDtypeStruct((m, D_MODEL), F32)
    return pl.pallas_call(
        _kvq_kernel,
        grid=(m // tm,),
        in_specs=[row(D_MODEL), _full((1, D_MODEL)), _full((1, D_MODEL)), _full(wkv.shape), _full(wb.shape),
                  row(LANES), row(LANES), row(LANES)],
        out_specs=[row(D_MODEL)] * 4,
        out_shape=[out] * 4,
        compiler_params=_params("arbitrary"),
        name="kvq_proj",
    )(h, gkv, gb, wkv, wb, cos, sup, sdn)


def _rope_tables(pos):
    inv_freq = ROPE_THETA ** (-jnp.arange(0, ROT_DIM, 2, dtype=F32) / ROT_DIM)
    ang = pos[:, None] * inv_freq[None, :]
    cos, sin = jnp.cos(ang), jnp.sin(ang)
    half = ROT_DIM // 2
    rows = pos.shape[0]
    pad = jnp.zeros((rows, HEAD_DIM - ROT_DIM), F32)
    zero = jnp.zeros((rows, half), F32)
    cos64 = jnp.concatenate([cos, cos, pad + 1.0], axis=1)
    sup64 = jnp.concatenate([zero, sin, pad], axis=1)
    sdn64 = jnp.concatenate([-sin, zero, pad], axis=1)
    return tuple(jnp.tile(t, (1, LANES // HEAD_DIM)) for t in (cos64, sup64, sdn64))


def _lambda(lq1_ref, lk1_ref, lq2_ref, lk2_ref):
    s1 = jnp.sum(lq1_ref[...] * lk1_ref[...], axis=1, keepdims=True)
    s2 = jnp.sum(lq2_ref[...] * lk2_ref[...], axis=1, keepdims=True)
    return jnp.exp(s1) - jnp.exp(s2) + LAMBDA_INIT


def _sub_norm(o, subln):
    return _rms(o) * subln * (1.0 - LAMBDA_INIT)


def _flash_kernel(q_ref, k_ref, v_ref, lq1_ref, lk1_ref, lq2_ref, lk2_ref, subln_ref, o_ref, *, tq):
    qi = pl.program_id(2)
    lane = lax.broadcasted_iota(jnp.int32, (tq, LANES), 1)
    qs = q_ref[0] * (HEAD_DIM ** -0.5)
    qq = jnp.concatenate([jnp.where(lane < HEAD_DIM, qs, 0.0), jnp.where(lane < HEAD_DIM, 0.0, qs)],
                         axis=0).astype(BF16)
    row_pos = lax.broadcasted_iota(jnp.int32, (2 * tq, tq), 0) % tq
    col_pos = lax.broadcasted_iota(jnp.int32, (2 * tq, tq), 1)

    def body(kb, carry):
        m, l, acc = carry
        start = pl.multiple_of(kb * tq, tq)
        kblk = k_ref[0, pl.ds(start, tq), :].astype(BF16)
        vblk = v_ref[0, pl.ds(start, tq), :].astype(BF16)
        s = lax.dot_general(qq, kblk, NT_DIMS, preferred_element_type=F32)
        s = jnp.where(col_pos + (kb - qi) * tq <= row_pos, s, NEG_INF)
        m_new = jnp.maximum(m, jnp.max(s, axis=1, keepdims=True))
        alpha = jnp.exp(m - m_new)
        p = jnp.exp(s - m_new)
        l = alpha * l + jnp.sum(p, axis=1, keepdims=True)
        acc = alpha * acc + _dot(p.astype(BF16), vblk)
        return m_new, l, acc

    init = (jnp.full((2 * tq, 1), NEG_INF, F32), jnp.zeros((2 * tq, 1), F32), jnp.zeros((2 * tq, LANES), F32))
    _, l, acc = lax.fori_loop(0, qi + 1, body, init)
    on = acc / l
    o = on[:tq] - _lambda(lq1_ref, lk1_ref, lq2_ref, lk2_ref) * on[tq:]
    o_ref[0] = _sub_norm(o, subln_ref[...])


def _flash(q, k, v, lq1, lk1, lq2, lk2, subln, tq):
    b, l, _ = q.shape
    head = pl.BlockSpec((1, l, LANES), lambda i, h, t: (i, 0, h))
    blk = pl.BlockSpec((1, tq, LANES), lambda i, h, t: (i, t, h))
    small = _full((1, HEAD_DIM))
    return pl.pallas_call(
        functools.partial(_flash_kernel, tq=tq),
        grid=(b, ATT_HEADS, l // tq),
        in_specs=[blk, head, head, small, small, small, small, _full((1, V_DIM))],
        out_specs=blk,
        out_shape=jax.ShapeDtypeStruct((b, l, VAL_WIDTH), F32),
        compiler_params=_params("arbitrary", "arbitrary", "arbitrary"),
        name="flash_diff",
    )(q, k, v, lq1, lk1, lq2, lk2, subln)


PAGES_PER_STEP = 8
N_MAPS = 2 * ATT_HEADS


def _paged_kernel(pt_ref, q_ref, kn_ref, vn_ref, lq1_ref, lk1_ref, lq2_ref, lk2_ref, subln_ref, *rest):
    k_refs = rest[:PAGES_PER_STEP]
    v_refs = rest[PAGES_PER_STEP:2 * PAGES_PER_STEP]
    o_ref = rest[2 * PAGES_PER_STEP]
    qrow_ref, m_ref, l_ref, acc_ref = rest[2 * PAGES_PER_STEP + 1:]
    step = pl.program_id(1)
    map_i = lax.broadcasted_iota(jnp.int32, (N_MAPS, K_DIM), 0)
    lane_map = lax.broadcasted_iota(jnp.int32, (N_MAPS, K_DIM), 1) // HEAD_DIM

    @pl.when(step == 0)
    def _():
        qs = q_ref[0] * (HEAD_DIM ** -0.5)
        qrow = jnp.where(lane_map == map_i, jnp.broadcast_to(qs, (N_MAPS, K_DIM)), 0.0)
        qrow_ref[...] = qrow.astype(BF16)
        m_ref[...] = jnp.sum(qrow * kn_ref[0], axis=1, keepdims=True)
        l_ref[...] = jnp.ones_like(l_ref)
        acc_ref[...] = jnp.broadcast_to(vn_ref[0], (N_MAPS, VAL_WIDTH))

    qrow = qrow_ref[...]
    for i in range(PAGES_PER_STEP):
        kb = k_refs[i][0].astype(BF16)
        vb = v_refs[i][0].astype(BF16)
        s = lax.dot_general(qrow, kb, NT_DIMS, preferred_element_type=F32)
        m_old = m_ref[...]
        m_new = jnp.maximum(m_old, jnp.max(s, axis=1, keepdims=True))
        alpha = jnp.exp(m_old - m_new)
        p = jnp.exp(s - m_new)
        l_ref[...] = alpha * l_ref[...] + jnp.sum(p, axis=1, keepdims=True)
        acc_ref[...] = alpha * acc_ref[...] + _dot(p.astype(BF16), vb)
        m_ref[...] = m_new

    @pl.when(step == pl.num_programs(1) - 1)
    def _():
        lam = _lambda(lq1_ref, lk1_ref, lq2_ref, lk2_ref)
        sub_i = lax.broadcasted_iota(jnp.int32, (N_MAPS, 1), 0)
        coef = jnp.where(sub_i % 2 == 0, 1.0, -lam) / l_ref[...]
        own = (lax.broadcasted_iota(jnp.int32, (N_MAPS, VAL_WIDTH), 1) // V_DIM
               == lax.broadcasted_iota(jnp.int32, (N_MAPS, VAL_WIDTH), 0) // 2)
        o = jnp.sum(jnp.where(own, acc_ref[...] * coef, 0.0), axis=0, keepdims=True)
        parts = [_sub_norm(o[:, h * V_DIM:(h + 1) * V_DIM], subln_ref[...]) for h in range(ATT_HEADS)]
        o_ref[0] = jnp.concatenate(parts, axis=1)


def _paged(page_table, q, kn, vn, cache_k, cache_v, lq1, lk1, lq2, lk2, subln):
    m = q.shape[0]
    n_steps = N_PAGES // PAGES_PER_STEP
    row = pl.BlockSpec((1, 1, D_MODEL), lambda i, s, pt: (i, 0, 0))
    small = pl.BlockSpec((1, HEAD_DIM), lambda i, s, pt: (0, 0))

    def page(r):
        return pl.BlockSpec((1, PAGE_SIZE, K_DIM),
                            lambda i, s, pt: (pt[i * N_PAGES + s * PAGES_PER_STEP + r], 0, 0))

    pages = [page(r) for r in range(PAGES_PER_STEP)]
    grid_spec = pltpu.PrefetchScalarGridSpec(
        num_scalar_prefetch=1,
        grid=(m, n_steps),
        in_specs=[row, row, row, small, small, small, small,
                  pl.BlockSpec((1, V_DIM), lambda i, s, pt: (0, 0))] + pages + pages,
        out_specs=row,
        scratch_shapes=[pltpu.VMEM((N_MAPS, K_DIM), BF16),
                        pltpu.VMEM((N_MAPS, 1), F32),
                        pltpu.VMEM((N_MAPS, 1), F32),
                        pltpu.VMEM((N_MAPS, VAL_WIDTH), F32)],
    )
    out = pl.pallas_call(
        _paged_kernel,
        grid_spec=grid_spec,
        out_shape=jax.ShapeDtypeStruct((m, 1, VAL_WIDTH), F32),
        compiler_params=_params("arbitrary", "arbitrary"),
        name="paged_diff",
    )(page_table.reshape(-1), q.reshape(m, 1, K_DIM), kn.reshape(m, 1, K_DIM), vn.reshape(m, 1, VAL_WIDTH),
      lq1, lk1, lq2, lk2, subln, *([cache_k] * PAGES_PER_STEP), *([cache_v] * PAGES_PER_STEP))
    return out.reshape(m, VAL_WIDTH)


def _attn_out_kernel(o_ref, gate_ref, h_ref, wout_ref, gf_ref, y_ref):
    og = (o_ref[...] * _silu(gate_ref[...])).astype(BF16)
    y_ref[...] = _rms(h_ref[...] + _dot(og, wout_ref[...])) * gf_ref[...]


def _attn_out(o, gate, h, wout, gf, tm):
    m = o.shape[0]
    row = pl.BlockSpec((tm, D_MODEL), lambda i: (i, 0))
    return pl.pallas_call(
        _attn_out_kernel,
        grid=(m // tm,),
        in_specs=[row, row, row, _full(wout.shape), _full((1, D_MODEL))],
        out_specs=row,
        out_shape=jax.ShapeDtypeStruct((m, D_MODEL), F32),
        compiler_params=_params("arbitrary"),
        name="attn_out",
    )(o, gate, h, wout, gf)


def _pad_lanes(v):
    return jnp.pad(v.reshape(1, -1), ((0, 0), (0, LANES - v.shape[-1])))


def kernel(x_prompt, x_sample, cache_k, cache_v, page_table, state_conv, state_ssm, norm_a, w_in_a, conv_w, conv_b, dt_bias, a_log, d_skip, gnorm_a, w_out_a, norm_kv, w_kv, norm_b, w_in_b, lambda_q1, lambda_k1, lambda_q2, lambda_k2, subln_b, w_out_b, norm_f):
    bp, lp, _ = x_prompt.shape
    bs = x_sample.shape[0]
    mp = bp * lp

    w_a = w_in_a[0].astype(BF16)
    wz, wx = w_a[:, :D_INNER], w_a[:, D_INNER:D_INNER + CONV_DIM]
    wdt = jnp.pad(w_a[:, D_INNER + CONV_DIM:], ((0, 0), (0, LANES - SSM_HEADS)))
    wout_a = w_out_a[0].astype(BF16)
    wkv = w_kv.astype(BF16)
    wb = w_in_b[0].astype(BF16)
    wout_b = w_out_b[0].astype(BF16)
    g_a = norm_a[0].reshape(1, D_MODEL)
    convw, convb = conv_w[0], conv_b[0].reshape(1, CONV_DIM)
    dtb, alog = _pad_lanes(dt_bias[0]), _pad_lanes(a_log[0])
    dskip = jnp.repeat(d_skip[0], SSM_HEAD_DIM).reshape(1, D_INNER)
    gnorm = gnorm_a[0].reshape(1, D_INNER)
    g_kv, g_b, g_f = norm_kv.reshape(1, D_MODEL), norm_b[0].reshape(1, D_MODEL), norm_f.reshape(1, D_MODEL)
    lq1, lk1, lq2, lk2 = (t[0].reshape(1, HEAD_DIM) for t in (lambda_q1, lambda_k1, lambda_q2, lambda_k2))
    subln = subln_b[0].reshape(1, V_DIM)
    head_of_lane = jnp.arange(D_INNER) // SSM_HEAD_DIM
    hexp = (jnp.arange(LANES)[:, None] == head_of_lane[None, :]).astype(F32)
    gexp = (jnp.arange(N_GROUPS)[:, None] == (head_of_lane // (SSM_HEADS // N_GROUPS))[None, :]).astype(F32)

    xp = x_prompt.reshape(mp, D_MODEL)
    z, xbc, dt = _inproj_a(xp, g_a, wz, wx, wdt, tm=256)
    h1, conv_p, ssm_p = _ssd_prompt(xbc.reshape(bp, lp, CONV_DIM), dt.reshape(bp, lp, LANES),
                                    z.reshape(bp, lp, D_INNER), x_prompt,
                                    convw, convb, dtb, alog, dskip, gnorm, wout_a)
    h1 = h1.reshape(mp, D_MODEL)
    tables_p = _rope_tables(jnp.tile(jnp.arange(lp, dtype=F32), bp))
    k_p, v_p, q_p, gate_p = _kvq(h1, g_kv, g_b, wkv, wb, *tables_p, tm=256)
    o_p = _flash(q_p.reshape(bp, lp, K_DIM), k_p.reshape(bp, lp, K_DIM), v_p.reshape(bp, lp, VAL_WIDTH),
                 lq1, lk1, lq2, lk2, subln, tq=256)
    y_p = _attn_out(o_p.reshape(mp, VAL_WIDTH), gate_p, h1, wout_b, g_f, tm=256)

    xs_in = x_sample.reshape(bs, D_MODEL)
    z_s, xbc_s, dt_s = _inproj_a(xs_in, g_a, wz, wx, wdt, tm=bs)
    conv_s, xdt, dec, bm, cm, ydiag, xs_act = _ssd_step_pre(xbc_s, dt_s, state_conv[0], convw, convb, dtb, alog,
                                                            hexp, gexp)
    yoff, ssm_s = _ssd_step_state(xdt, dec, bm, cm, state_ssm[0])
    h1_s = _ssd_step_out(ydiag, yoff, dec, xs_act, z_s, xs_in, dskip, gnorm, wout_a)
    tables_s = _rope_tables(jnp.full((bs,), float(PAST_LEN), F32))
    k_s, v_s, q_s, gate_s = _kvq(h1_s, g_kv, g_b, wkv, wb, *tables_s, tm=bs)
    o_s = _paged(page_table, q_s, k_s, v_s,
                 cache_k.reshape(cache_k.shape[0], PAGE_SIZE, K_DIM),
                 cache_v.reshape(cache_v.shape[0], PAGE_SIZE, VAL_WIDTH),
                 lq1, lk1, lq2, lk2, subln)
    y_s = _attn_out(o_s, gate_s, h1_s, wout_b, g_f, tm=bs)

    return (y_p.reshape(bp, lp, D_MODEL), y_s.reshape(bs, 1, D_MODEL),
            k_p.reshape(bp, lp, ATT_HEADS, 2, HEAD_DIM), v_p.reshape(bp, lp, ATT_HEADS, V_DIM),
            conv_p[None], ssm_p[None],
            k_s.reshape(bs, 1, ATT_HEADS, 2, HEAD_DIM), v_s.reshape(bs, 1, ATT_HEADS, V_DIM),
            conv_s[None], ssm_s[None])
```

```python
import functools
import math

import jax
import jax.numpy as jnp
from jax import lax
from jax.experimental import pallas as pl
from jax.experimental.pallas import tpu as pltpu

F32 = jnp.float32
BF16 = jnp.bfloat16
HIGHEST = lax.Precision.HIGHEST

D_MODEL = 1024
PAST_LEN = 8192
PAGE_SIZE = 128
N_PAGES = PAST_LEN // PAGE_SIZE
D_INNER = 2048
SSM_HEAD_DIM = 64
SSM_HEADS = 32
N_GROUPS = 4
D_STATE = 128
D_CONV = 4
CONV_DIM = D_INNER + 2 * N_GROUPS * D_STATE
CHUNK = 128
ATT_HEADS = 8
HEAD_DIM = 64
V_DIM = 128
K_DIM = 1024
VAL_WIDTH = 1024
ROT_DIM = 16
ROPE_THETA = 500000.0
EPS = 1e-6
NEG_INF = -1e30
LAMBDA_INIT = 0.8 - 0.6 * math.exp(-0.3 * 1)

LANES = 128
VMEM_LIMIT = 56 * 1024 * 1024

NT_DIMS = (((1,), (1,)), ((), ()))
TN_DIMS = (((0,), (0,)), ((), ()))


def _params(*sem):
    return pltpu.CompilerParams(dimension_semantics=sem, vmem_limit_bytes=VMEM_LIMIT)


def _silu(x):
    return x * (1.0 / (1.0 + jnp.exp(-x)))


def _softplus(x):
    return jnp.maximum(x, 0.0) + jnp.log1p(jnp.exp(-jnp.abs(x)))


def _rms(x):
    return x * lax.rsqrt(jnp.mean(x * x, axis=-1, keepdims=True) + EPS)


def _dot(a, b):
    return jnp.dot(a, b, preferred_element_type=F32)


def _full(shape):
    return pl.BlockSpec(shape, lambda *_: (0,) * len(shape))


def _inproj_a_kernel(x_ref, g_ref, wz_ref, wx_ref, wdt_ref, z_ref, xbc_ref, dt_ref):
    xn = (_rms(x_ref[...]) * g_ref[...]).astype(BF16)
    z_ref[...] = _dot(xn, wz_ref[...])
    xbc_ref[...] = _dot(xn, wx_ref[...])
    dt_ref[...] = _dot(xn, wdt_ref[...])


def _inproj_a(x, g, wz, wx, wdt, tm):
    m = x.shape[0]
    row = lambda n: pl.BlockSpec((tm, n), lambda i: (i, 0))
    return pl.pallas_call(
        _inproj_a_kernel,
        grid=(m // tm,),
        in_specs=[row(D_MODEL), _full((1, D_MODEL)), _full(wz.shape), _full(wx.shape), _full(wdt.shape)],
        out_specs=[row(D_INNER), row(CONV_DIM), row(LANES)],
        out_shape=[jax.ShapeDtypeStruct((m, D_INNER), F32),
                   jax.ShapeDtypeStruct((m, CONV_DIM), F32),
                   jax.ShapeDtypeStruct((m, LANES), F32)],
        compiler_params=_params("arbitrary"),
        name="inproj_a",
    )(x, g, wz, wx, wdt)


def _gate_norm_outproj(y, z, gnorm, wout, xres):
    yg = y * _silu(z)
    gw = D_INNER // N_GROUPS
    parts = [_rms(yg[:, g * gw:(g + 1) * gw]) for g in range(N_GROUPS)]
    yn = (jnp.concatenate(parts, axis=1) * gnorm).astype(BF16)
    return xres + _dot(yn, wout)


def _ssd_prompt_kernel(xbc_ref, dt_ref, z_ref, x_ref, convw_ref, convb_ref, dtb_ref, alog_ref,
                       dskip_ref, gnorm_ref, wout_ref,
                       h_ref, conv_out_ref, ssm_out_ref,
                       xpad_ref, act_ref, y_ref, st_ref):
    c = pl.program_id(1)
    last = pl.num_programs(1) - 1
    q = CHUNK

    @pl.when(c == 0)
    def _():
        xpad_ref[0:8, :] = jnp.zeros((8, CONV_DIM), F32)
        st_ref[...] = jnp.zeros_like(st_ref)

    xbc = xbc_ref[0]
    xpad_ref[8:8 + q, :] = xbc
    conv = convb_ref[...] + xbc * convw_ref[3:4, :]
    for k in range(D_CONV - 1):
        conv = conv + xpad_ref[5 + k:5 + k + q, :] * convw_ref[k:k + 1, :]
    act_ref[...] = _silu(conv)
    xpad_ref[0:8, :] = xpad_ref[q:q + 8, :]

    @pl.when(c == last)
    def _():
        conv_out_ref[0] = xpad_ref[5:8, :]

    dt = _softplus(dt_ref[0] + dtb_ref[...])
    a = -jnp.exp(alog_ref[...])
    row_i = lax.broadcasted_iota(jnp.int32, (q, q), 0)
    col_i = lax.broadcasted_iota(jnp.int32, (q, q), 1)
    causal = row_i >= col_i
    a_cs = jnp.dot(causal.astype(F32), dt * a, precision=HIGHEST, preferred_element_type=F32)
    a_cs_t = a_cs.T
    dt_t = dt.T
    w_t = dt_t * jnp.exp(a_cs_t[:, q - 1:q] - a_cs_t)
    e_cs = jnp.exp(a_cs)
    first_half = col_i < SSM_HEAD_DIM

    for g in range(N_GROUPS):
        bm = act_ref[:, D_INNER + g * D_STATE:D_INNER + (g + 1) * D_STATE]
        cm = act_ref[:, D_INNER + (N_GROUPS + g) * D_STATE:D_INNER + (N_GROUPS + g + 1) * D_STATE]
        bm_b = bm.astype(BF16)
        cm_b = cm.astype(BF16)
        cb = lax.dot_general(cm_b, bm_b, NT_DIMS, preferred_element_type=F32)
        bm_t = bm.T
        for jj in range(SSM_HEADS // N_GROUPS // 2):
            j = g * (SSM_HEADS // N_GROUPS // 2) + jj
            xs = act_ref[:, j * LANES:(j + 1) * LANES]
            xs_b = xs.astype(BF16)
            zero = jnp.zeros_like(xs_b)
            xbd = jnp.concatenate([jnp.where(first_half, xs_b, zero),
                                   jnp.where(first_half, zero, xs_b)], axis=0)
            m_parts, w_parts = [], []
            for h in (2 * j, 2 * j + 1):
                seg = a_cs[:, h:h + 1] - a_cs_t[h:h + 1, :]
                decay = jnp.exp(jnp.where(causal, seg, -jnp.inf))
                m_parts.append((cb * decay * dt_t[h:h + 1, :]).astype(BF16))
                w_parts.append((bm_t * w_t[h:h + 1, :]).astype(BF16))
            y_diag = _dot(jnp.concatenate(m_parts, axis=1), xbd)
            upd_t = _dot(jnp.concatenate(w_parts, axis=1), xbd)
            st = st_ref[j]
            e_sel = jnp.where(first_half, e_cs[:, 2 * j:2 * j + 1], e_cs[:, 2 * j + 1:2 * j + 2])
            y_off = _dot(cm_b, st.astype(BF16)) * e_sel
            st_new = st * e_sel[q - 1:q, :] + upd_t
            st_ref[j] = st_new
            y_ref[:, j * LANES:(j + 1) * LANES] = y_diag + y_off + dskip_ref[:, j * LANES:(j + 1) * LANES] * xs

            @pl.when(c == last)
            def _():
                ssm_out_ref[0, 2 * j:2 * j + 2] = st_new.T.reshape(2, SSM_HEAD_DIM, D_STATE)

    h_ref[0] = _gate_norm_outproj(y_ref[...], z_ref[0], gnorm_ref[...], wout_ref[...], x_ref[0])


def _ssd_prompt(xbc, dt, z, x, convw, convb, dtb, alog, dskip, gnorm, wout):
    b, l, _ = x.shape
    nc = l // CHUNK
    tok = lambda n: pl.BlockSpec((1, CHUNK, n), lambda i, c: (i, c, 0))
    return pl.pallas_call(
        _ssd_prompt_kernel,
        grid=(b, nc),
        in_specs=[tok(CONV_DIM), tok(LANES), tok(D_INNER), tok(D_MODEL),
                  _full(convw.shape), _full(convb.shape), _full(dtb.shape), _full(alog.shape),
                  _full(dskip.shape), _full(gnorm.shape), _full(wout.shape)],
        out_specs=[tok(D_MODEL),
                   pl.BlockSpec((1, D_CONV - 1, CONV_DIM), lambda i, c: (i, 0, 0)),
                   pl.BlockSpec((1, SSM_HEADS, SSM_HEAD_DIM, D_STATE), lambda i, c: (i, 0, 0, 0))],
        out_shape=[jax.ShapeDtypeStruct((b, l, D_MODEL), F32),
                   jax.ShapeDtypeStruct((b, D_CONV - 1, CONV_DIM), F32),
                   jax.ShapeDtypeStruct((b, SSM_HEADS, SSM_HEAD_DIM, D_STATE), F32)],
        scratch_shapes=[pltpu.VMEM((CHUNK + 8, CONV_DIM), F32),
                        pltpu.VMEM((CHUNK, CONV_DIM), F32),
                        pltpu.VMEM((CHUNK, D_INNER), F32),
                        pltpu.VMEM((SSM_HEADS // 2, D_STATE, LANES), F32)],
        compiler_params=_params("arbitrary", "arbitrary"),
        name="ssd_prompt",
    )(xbc, dt, z, x, convw, convb, dtb, alog, dskip, gnorm, wout)


def _ssd_step_pre_kernel(xbc_ref, dt_ref, sconv_ref, convw_ref, convb_ref, dtb_ref, alog_ref,
                         hexp_ref, gexp_ref,
                         conv_out_ref, xdt_ref, dec_ref, bm_ref, cm_ref, ydiag_ref, xs_ref):
    xbc = xbc_ref[...]
    conv = convb_ref[...] + xbc * convw_ref[3:4, :]
    for k in range(D_CONV - 1):
        conv = conv + sconv_ref[k] * convw_ref[k:k + 1, :]
    conv_out_ref[0] = sconv_ref[1]
    conv_out_ref[1] = sconv_ref[2]
    conv_out_ref[2] = xbc
    act = _silu(conv)
    xs = act[:, :D_INNER]
    bm = act[:, D_INNER:D_INNER + N_GROUPS * D_STATE]
    cm = act[:, D_INNER + N_GROUPS * D_STATE:]
    dt = _softplus(dt_ref[...] + dtb_ref[...])
    dec = jnp.exp(dt * -jnp.exp(alog_ref[...]))
    hexp = hexp_ref[...]
    dt_x = jnp.dot(dt, hexp, precision=HIGHEST, preferred_element_type=F32)
    dec_x = jnp.dot(dec, hexp, precision=HIGHEST, preferred_element_type=F32)
    prod = bm * cm
    cb = jnp.concatenate([jnp.sum(prod[:, g * D_STATE:(g + 1) * D_STATE], axis=1, keepdims=True)
                          for g in range(N_GROUPS)], axis=1)
    cb_x = jnp.dot(cb, gexp_ref[...], precision=HIGHEST, preferred_element_type=F32)
    xdt = xs * dt_x
    xdt_ref[...] = xdt
    dec_ref[...] = dec_x
    bm_ref[...] = bm
    cm_ref[...] = cm
    ydiag_ref[...] = cb_x * xdt
    xs_ref[...] = xs


def _ssd_step_pre(xbc, dt, sconv, convw, convb, dtb, alog, hexp, gexp):
    m = xbc.shape[0]
    gs = N_GROUPS * D_STATE
    return pl.pallas_call(
        _ssd_step_pre_kernel,
        out_shape=[jax.ShapeDtypeStruct((D_CONV - 1, m, CONV_DIM), F32),
                   jax.ShapeDtypeStruct((m, D_INNER), F32),
                   jax.ShapeDtypeStruct((m, D_INNER), F32),
                   jax.ShapeDtypeStruct((m, gs), F32),
                   jax.ShapeDtypeStruct((m, gs), F32),
                   jax.ShapeDtypeStruct((m, D_INNER), F32),
                   jax.ShapeDtypeStruct((m, D_INNER), F32)],
        compiler_params=pltpu.CompilerParams(vmem_limit_bytes=VMEM_LIMIT),
        name="ssd_step_pre",
    )(xbc, dt, sconv, convw, convb, dtb, alog, hexp, gexp)


def _ssd_step_state_kernel(xdt_ref, dec_ref, bm_ref, cm_ref, st_ref, yoff_ref, st_out_ref):
    gw = D_INNER // N_GROUPS
    hg = SSM_HEADS // N_GROUPS
    sub = lax.broadcasted_iota(jnp.int32, (8, gw), 0)
    sub_n = lax.broadcasted_iota(jnp.int32, (8, D_STATE), 0)
    ones_row = jnp.where(sub_n == 0, 1.0, 0.0).astype(F32)
    for g in range(N_GROUPS):
        xdt8 = jnp.where(sub == 0, xdt_ref[0, :, g * gw:(g + 1) * gw], 0.0)
        dec8 = jnp.where(sub == 0, dec_ref[0, :, g * gw:(g + 1) * gw], 0.0)
        bm8 = jnp.where(sub_n == 0, bm_ref[0, :, g * D_STATE:(g + 1) * D_STATE], 0.0)
        cm8 = jnp.broadcast_to(cm_ref[0, :, g * D_STATE:(g + 1) * D_STATE], (8, D_STATE)).astype(BF16)
        st = st_ref[0, g * hg:(g + 1) * hg].reshape(gw, D_STATE)
        y8 = lax.dot_general(cm8, st.astype(BF16), NT_DIMS, preferred_element_type=F32)
        yoff_ref[0, :, g * gw:(g + 1) * gw] = y8[0:1, :]
        dec_col = lax.dot_general(dec8, ones_row, TN_DIMS, precision=HIGHEST, preferred_element_type=F32)
        upd = lax.dot_general(xdt8, bm8, TN_DIMS, precision=HIGHEST, preferred_element_type=F32)
        st_out_ref[0, g * hg:(g + 1) * hg] = (st * dec_col + upd).reshape(hg, SSM_HEAD_DIM, D_STATE)


def _ssd_step_state(xdt, dec, bm, cm, st):
    m = xdt.shape[0]
    gs = N_GROUPS * D_STATE
    row = lambda n: pl.BlockSpec((1, 1, n), lambda i: (i, 0, 0))
    st_spec = pl.BlockSpec((1, SSM_HEADS, SSM_HEAD_DIM, D_STATE), lambda i: (i, 0, 0, 0))
    yoff, st_out = pl.pallas_call(
        _ssd_step_state_kernel,
        grid=(m,),
        in_specs=[row(D_INNER), row(D_INNER), row(gs), row(gs), st_spec],
        out_specs=[row(D_INNER), st_spec],
        out_shape=[jax.ShapeDtypeStruct((m, 1, D_INNER), F32),
                   jax.ShapeDtypeStruct(st.shape, F32)],
        compiler_params=_params("arbitrary"),
        name="ssd_step_state",
    )(xdt.reshape(m, 1, D_INNER), dec.reshape(m, 1, D_INNER), bm.reshape(m, 1, gs), cm.reshape(m, 1, gs), st)
    return yoff.reshape(m, D_INNER), st_out


def _ssd_step_out_kernel(ydiag_ref, yoff_ref, dec_ref, xs_ref, z_ref, x_ref, dskip_ref, gnorm_ref, wout_ref,
                         h_ref):
    y = ydiag_ref[...] + yoff_ref[...] * dec_ref[...] + dskip_ref[...] * xs_ref[...]
    h_ref[...] = _gate_norm_outproj(y, z_ref[...], gnorm_ref[...], wout_ref[...], x_ref[...])


def _ssd_step_out(ydiag, yoff, dec, xs, z, x, dskip, gnorm, wout):
    return pl.pallas_call(
        _ssd_step_out_kernel,
        out_shape=jax.ShapeDtypeStruct(x.shape, F32),
        compiler_params=pltpu.CompilerParams(vmem_limit_bytes=VMEM_LIMIT),
        name="ssd_step_out",
    )(ydiag, yoff, dec, xs, z, x, dskip, gnorm, wout)


def _rope(t, cos, sin_up, sin_dn):
    cols = []
    for c in range(t.shape[1] // LANES):
        tc = t[:, c * LANES:(c + 1) * LANES]
        cols.append(tc * cos + pltpu.roll(tc, ROT_DIM // 2, 1) * sin_up
                    + pltpu.roll(tc, LANES - ROT_DIM // 2, 1) * sin_dn)
    return jnp.concatenate(cols, axis=1)


def _kvq_kernel(h_ref, gkv_ref, gb_ref, wkv_ref, wb_ref, cos_ref, sup_ref, sdn_ref, *out_refs, for_flash):
    xh = _rms(h_ref[...])
    kv = _dot((xh * gkv_ref[...]).astype(BF16), wkv_ref[...])
    qg = _dot((xh * gb_ref[...]).astype(BF16), wb_ref[...])
    cos, sup, sdn = cos_ref[...], sup_ref[...], sdn_ref[...]
    k = _rope(kv[:, :K_DIM], cos, sup, sdn)
    v = kv[:, K_DIM:]
    q = _rope(qg[:, :K_DIM], cos, sup, sdn)
    if for_flash:
        kt_ref, v_ref, kb_ref, vt_ref, qt_ref, gate_ref = out_refs
        kt_ref[0] = k.T
        kb_ref[...] = k.astype(BF16)
        vt_ref[0, 0] = v.T.astype(BF16)
        qt_ref[0] = (q * (HEAD_DIM ** -0.5)).T.astype(BF16)
    else:
        k_ref, v_ref, q_ref, gate_ref = out_refs
        k_ref[...] = k
        q_ref[...] = q
    v_ref[...] = v
    gate_ref[...] = qg[:, K_DIM:]


def _kvq_in_specs(tm, wkv, wb, table_blocks):
    table = pl.BlockSpec((tm, LANES), lambda i: (i % table_blocks, 0))
    return [pl.BlockSpec((tm, D_MODEL), lambda i: (i, 0)), _full((1, D_MODEL)), _full((1, D_MODEL)),
            _full(wkv.shape), _full(wb.shape), table, table, table]


def _kvq(h, gkv, gb, wkv, wb, cos, sup, sdn, tm):
    m = h.shape[0]
    row = pl.BlockSpec((tm, D_MODEL), lambda i: (i, 0))
    out = jax.ShapeDtypeStruct((m, D_MODEL), F32)
    return pl.pallas_call(
        functools.partial(_kvq_kernel, for_flash=False),
        grid=(m // tm,),
        in_specs=_kvq_in_specs(tm, wkv, wb, cos.shape[0] // tm),
        out_specs=[row] * 4,
        out_shape=[out] * 4,
        compiler_params=_params("arbitrary"),
        name="kvq_proj",
    )(h, gkv, gb, wkv, wb, cos, sup, sdn)


FLASH_TK = 256


def _kvq_flash(h, gkv, gb, wkv, wb, cos, sup, sdn, b, l):
    tm = FLASH_TK
    nb = l // tm
    row = pl.BlockSpec((tm, D_MODEL), lambda i: (i, 0))
    feat = pl.BlockSpec((1, D_MODEL, tm), lambda i: (i // nb, 0, i % nb))
    return pl.pallas_call(
        functools.partial(_kvq_kernel, for_flash=True),
        grid=(b * nb,),
        in_specs=_kvq_in_specs(tm, wkv, wb, nb),
        out_specs=[feat, row, row, pl.BlockSpec((1, 1, D_MODEL, tm), lambda i: (i // nb, i % nb, 0, 0)), feat, row],
        out_shape=[jax.ShapeDtypeStruct((b, K_DIM, l), F32),
                   jax.ShapeDtypeStruct((b * l, VAL_WIDTH), F32),
                   jax.ShapeDtypeStruct((b * l, K_DIM), BF16),
                   jax.ShapeDtypeStruct((b, nb, VAL_WIDTH, tm), BF16),
                   jax.ShapeDtypeStruct((b, K_DIM, l), BF16),
                   jax.ShapeDtypeStruct((b * l, VAL_WIDTH), F32)],
        compiler_params=_params("arbitrary"),
        name="kvq_proj_flash",
    )(h, gkv, gb, wkv, wb, cos, sup, sdn)


def _rope_tables(pos):
    inv_freq = ROPE_THETA ** (-jnp.arange(0, ROT_DIM, 2, dtype=F32) / ROT_DIM)
    ang = pos[:, None] * inv_freq[None, :]
    cos, sin = jnp.cos(ang), jnp.sin(ang)
    half = ROT_DIM // 2
    rows = pos.shape[0]
    pad = jnp.zeros((rows, HEAD_DIM - ROT_DIM), F32)
    zero = jnp.zeros((rows, half), F32)
    cos64 = jnp.concatenate([cos, cos, pad + 1.0], axis=1)
    sup64 = jnp.concatenate([zero, sin, pad], axis=1)
    sdn64 = jnp.concatenate([-sin, zero, pad], axis=1)
    return tuple(jnp.tile(t, (1, LANES // HEAD_DIM)) for t in (cos64, sup64, sdn64))


def _lambda(lq1_ref, lk1_ref, lq2_ref, lk2_ref):
    s1 = jnp.sum(lq1_ref[...] * lk1_ref[...], axis=1, keepdims=True)
    s2 = jnp.sum(lq2_ref[...] * lk2_ref[...], axis=1, keepdims=True)
    return jnp.exp(s1) - jnp.exp(s2) + LAMBDA_INIT


def _sub_norm(o, subln):
    return _rms(o) * subln * (1.0 - LAMBDA_INIT)


def _flash_kernel(qt_ref, k_ref, vt_ref, lq1_ref, lk1_ref, lq2_ref, lk2_ref, subln_ref, o_ref, acc_ref, *, tq, tk):
    qi = pl.program_id(2)
    n = 2 * tq
    qt = qt_ref[0]
    feat = lax.broadcasted_iota(jnp.int32, (LANES, tq), 0)
    zero = jnp.zeros_like(qt)
    qqt = jnp.concatenate([jnp.where(feat < HEAD_DIM, qt, zero), jnp.where(feat < HEAD_DIM, zero, qt)], axis=1)
    acc_ref[...] = jnp.zeros_like(acc_ref)

    def step(kb, m, l, masked):
        start = pl.multiple_of(kb * tk, tk)
        st = _dot(k_ref[0, pl.ds(start, tk), :], qqt)
        if masked:
            k_pos = start + lax.broadcasted_iota(jnp.int32, (tk, n), 0)
            q_pos = qi * tq + lax.broadcasted_iota(jnp.int32, (tk, n), 1) % tq
            st = jnp.where(k_pos <= q_pos, st, NEG_INF)
        m_new = jnp.maximum(m, jnp.max(st, axis=0, keepdims=True))
        alpha = jnp.exp(m - m_new)
        p = jnp.exp(st - m_new)
        l = alpha * l + jnp.sum(p, axis=0, keepdims=True)
        acc_ref[...] = acc_ref[...] * alpha + _dot(vt_ref[0, kb], p.astype(BF16))
        return m_new, l

    n_full = qi * (tq // tk)
    m, l = lax.fori_loop(0, n_full, lambda kb, c: step(kb, c[0], c[1], False),
                         (jnp.full((1, n), NEG_INF, F32), jnp.zeros((1, n), F32)))
    for d in range(tq // tk):
        m, l = step(n_full + d, m, l, True)

    on = acc_ref[...] / l
    ot = on[:, :tq] - _lambda(lq1_ref, lk1_ref, lq2_ref, lk2_ref) * on[:, tq:]
    ot = ot * lax.rsqrt(jnp.mean(ot * ot, axis=0, keepdims=True) + EPS) * subln_ref[...] * (1.0 - LAMBDA_INIT)
    o_ref[0] = ot.T


def _flash(qt, kb, vt, lq1, lk1, lq2, lk2, subln_col, tq):
    b, _, l = qt.shape
    tk = FLASH_TK
    small = _full((1, HEAD_DIM))
    return pl.pallas_call(
        functools.partial(_flash_kernel, tq=tq, tk=tk),
        grid=(b, ATT_HEADS, l // tq),
        in_specs=[pl.BlockSpec((1, LANES, tq), lambda i, h, t: (i, h, t)),
                  pl.BlockSpec((1, l, LANES), lambda i, h, t: (i, 0, h)),
                  pl.BlockSpec((1, l // tk, LANES, tk), lambda i, h, t: (i, 0, h, 0)),
                  small, small, small, small, _full((V_DIM, 1))],
        out_specs=pl.BlockSpec((1, tq, LANES), lambda i, h, t: (i, t, h)),
        out_shape=jax.ShapeDtypeStruct((b, l, VAL_WIDTH), F32),
        scratch_shapes=[pltpu.VMEM((V_DIM, 2 * tq), F32)],
        compiler_params=_params("arbitrary", "arbitrary", "arbitrary"),
        name="flash_diff",
    )(qt, kb, vt, lq1, lk1, lq2, lk2, subln_col)


PAGES_PER_STEP = 8
N_MAPS = 2 * ATT_HEADS


def _paged_kernel(pt_ref, q_ref, kn_ref, vn_ref, lq1_ref, lk1_ref, lq2_ref, lk2_ref, subln_ref, *rest):
    kt_refs = rest[:PAGES_PER_STEP]
    v_refs = rest[PAGES_PER_STEP:2 * PAGES_PER_STEP]
    o_ref = rest[2 * PAGES_PER_STEP]
    qrow_ref, m_ref, l_ref, acc_ref = rest[2 * PAGES_PER_STEP + 1:]
    step = pl.program_id(1)
    head_of_map = lax.broadcasted_iota(jnp.int32, (N_MAPS, PAGE_SIZE), 0) % ATT_HEADS

    @pl.when(step == 0)
    def _():
        map_i = lax.broadcasted_iota(jnp.int32, (N_MAPS, K_DIM), 0)
        seg = 2 * (map_i % ATT_HEADS) + map_i // ATT_HEADS
        lane_seg = lax.broadcasted_iota(jnp.int32, (N_MAPS, K_DIM), 1) // HEAD_DIM
        qs = q_ref[0] * (HEAD_DIM ** -0.5)
        qrow = jnp.where(lane_seg == seg, jnp.broadcast_to(qs, (N_MAPS, K_DIM)), 0.0)
        qrow_ref[...] = qrow.astype(BF16)
        m_ref[...] = jnp.sum(qrow * kn_ref[0], axis=1, keepdims=True)
        l_ref[...] = jnp.ones_like(l_ref)
        acc_ref[...] = jnp.concatenate([vn_ref[0], vn_ref[0]], axis=0)

    qrow = qrow_ref[...]
    for i in range(PAGES_PER_STEP):
        s = _dot(qrow, kt_refs[i][0].astype(BF16))
        m_old = m_ref[...]
        m_new = jnp.maximum(m_old, jnp.max(s, axis=1, keepdims=True))
        alpha = jnp.exp(m_old - m_new)
        p = jnp.exp(s - m_new)
        l_ref[...] = alpha * l_ref[...] + jnp.sum(p, axis=1, keepdims=True)
        m_ref[...] = m_new
        p_heads = jnp.concatenate([jnp.where(head_of_map == h, p, 0.0) for h in range(ATT_HEADS)], axis=1)
        v_heads = jnp.concatenate([v_refs[i][0, pl.ds(h, PAGE_SIZE, stride=ATT_HEADS), :]
                                   for h in range(ATT_HEADS)], axis=0)
        acc_ref[...] = alpha * acc_ref[...] + _dot(p_heads.astype(BF16), v_heads.astype(BF16))

    @pl.when(step == pl.num_programs(1) - 1)
    def _():
        on = acc_ref[...] / l_ref[...]
        o = on[:ATT_HEADS] - _lambda(lq1_ref, lk1_ref, lq2_ref, lk2_ref) * on[ATT_HEADS:]
        o_ref[0] = _sub_norm(o, subln_ref[...])


def _paged(page_table, q, kn, vn, cache_kt, cache_v, lq1, lk1, lq2, lk2, subln):
    m = q.shape[0]
    n_steps = N_PAGES // PAGES_PER_STEP
    row = pl.BlockSpec((1, 1, D_MODEL), lambda i, s, pt: (i, 0, 0))
    heads = pl.BlockSpec((1, ATT_HEADS, V_DIM), lambda i, s, pt: (i, 0, 0))
    small = pl.BlockSpec((1, HEAD_DIM), lambda i, s, pt: (0, 0))

    def page(r):
        return pl.BlockSpec((1, K_DIM, PAGE_SIZE),
                            lambda i, s, pt: (pt[i * N_PAGES + s * PAGES_PER_STEP + r], 0, 0))

    pages = [page(r) for r in range(PAGES_PER_STEP)]
    grid_spec = pltpu.PrefetchScalarGridSpec(
        num_scalar_prefetch=1,
        grid=(m, n_steps),
        in_specs=[row, row, heads, small, small, small, small,
                  pl.BlockSpec((1, V_DIM), lambda i, s, pt: (0, 0))] + pages + pages,
        out_specs=heads,
        scratch_shapes=[pltpu.VMEM((N_MAPS, K_DIM), BF16),
                        pltpu.VMEM((N_MAPS, 1), F32),
                        pltpu.VMEM((N_MAPS, 1), F32),
                        pltpu.VMEM((N_MAPS, V_DIM), F32)],
    )
    out = pl.pallas_call(
        _paged_kernel,
        grid_spec=grid_spec,
        out_shape=jax.ShapeDtypeStruct((m, ATT_HEADS, V_DIM), F32),
        compiler_params=_params("arbitrary", "arbitrary"),
        name="paged_diff",
    )(page_table.reshape(-1), q.reshape(m, 1, K_DIM), kn.reshape(m, 1, K_DIM), vn.reshape(m, ATT_HEADS, V_DIM),
      lq1, lk1, lq2, lk2, subln, *([cache_kt] * PAGES_PER_STEP), *([cache_v] * PAGES_PER_STEP))
    return out.reshape(m, VAL_WIDTH)


def _attn_out_kernel(o_ref, gate_ref, h_ref, wout_ref, gf_ref, y_ref):
    og = (o_ref[...] * _silu(gate_ref[...])).astype(BF16)
    y_ref[...] = _rms(h_ref[...] + _dot(og, wout_ref[...])) * gf_ref[...]


def _attn_out(o, gate, h, wout, gf, tm):
    m = o.shape[0]
    row = pl.BlockSpec((tm, D_MODEL), lambda i: (i, 0))
    return pl.pallas_call(
        _attn_out_kernel,
        grid=(m // tm,),
        in_specs=[row, row, row, _full(wout.shape), _full((1, D_MODEL))],
        out_specs=row,
        out_shape=jax.ShapeDtypeStruct((m, D_MODEL), F32),
        compiler_params=_params("arbitrary"),
        name="attn_out",
    )(o, gate, h, wout, gf)


def _pad_lanes(v):
    return jnp.pad(v.reshape(1, -1), ((0, 0), (0, LANES - v.shape[-1])))


def kernel(x_prompt, x_sample, cache_k, cache_v, page_table, state_conv, state_ssm, norm_a, w_in_a, conv_w, conv_b, dt_bias, a_log, d_skip, gnorm_a, w_out_a, norm_kv, w_kv, norm_b, w_in_b, lambda_q1, lambda_k1, lambda_q2, lambda_k2, subln_b, w_out_b, norm_f):
    bp, lp, _ = x_prompt.shape
    bs = x_sample.shape[0]
    mp = bp * lp

    w_a = w_in_a[0].astype(BF16)
    wz, wx = w_a[:, :D_INNER], w_a[:, D_INNER:D_INNER + CONV_DIM]
    wdt = jnp.pad(w_a[:, D_INNER + CONV_DIM:], ((0, 0), (0, LANES - SSM_HEADS)))
    wout_a = w_out_a[0].astype(BF16)
    wkv = w_kv.astype(BF16)
    wb = w_in_b[0].astype(BF16)
    wout_b = w_out_b[0].astype(BF16)
    g_a = norm_a[0].reshape(1, D_MODEL)
    convw, convb = conv_w[0], conv_b[0].reshape(1, CONV_DIM)
    dtb, alog = _pad_lanes(dt_bias[0]), _pad_lanes(a_log[0])
    dskip = jnp.repeat(d_skip[0], SSM_HEAD_DIM).reshape(1, D_INNER)
    gnorm = gnorm_a[0].reshape(1, D_INNER)
    g_kv, g_b, g_f = norm_kv.reshape(1, D_MODEL), norm_b[0].reshape(1, D_MODEL), norm_f.reshape(1, D_MODEL)
    lq1, lk1, lq2, lk2 = (t[0].reshape(1, HEAD_DIM) for t in (lambda_q1, lambda_k1, lambda_q2, lambda_k2))
    subln = subln_b[0].reshape(1, V_DIM)
    head_of_lane = jnp.arange(D_INNER) // SSM_HEAD_DIM
    hexp = (jnp.arange(LANES)[:, None] == head_of_lane[None, :]).astype(F32)
    gexp = (jnp.arange(N_GROUPS)[:, None] == (head_of_lane // (SSM_HEADS // N_GROUPS))[None, :]).astype(F32)

    xp = x_prompt.reshape(mp, D_MODEL)
    z, xbc, dt = _inproj_a(xp, g_a, wz, wx, wdt, tm=256)
    h1, conv_p, ssm_p = _ssd_prompt(xbc.reshape(bp, lp, CONV_DIM), dt.reshape(bp, lp, LANES),
                                    z.reshape(bp, lp, D_INNER), x_prompt,
                                    convw, convb, dtb, alog, dskip, gnorm, wout_a)
    h1 = h1.reshape(mp, D_MODEL)
    tables_p = _rope_tables(jnp.arange(lp, dtype=F32))
    kt_p, v_p, kb_p, vt_p, qt_p, gate_p = _kvq_flash(h1, g_kv, g_b, wkv, wb, *tables_p, b=bp, l=lp)
    o_p = _flash(qt_p, kb_p.reshape(bp, lp, K_DIM), vt_p, lq1, lk1, lq2, lk2, subln.reshape(V_DIM, 1), tq=512)
    y_p = _attn_out(o_p.reshape(mp, VAL_WIDTH), gate_p, h1, wout_b, g_f, tm=256)
    k_p = jnp.transpose(kt_p.reshape(bp, ATT_HEADS, 2, HEAD_DIM, lp), (0, 4, 1, 2, 3))

    n_pool = cache_k.shape[0]
    cache_kt = jnp.transpose(cache_k, (0, 2, 3, 4, 1)).reshape(n_pool, K_DIM, PAGE_SIZE)
    cache_vr = cache_v.reshape(n_pool, PAGE_SIZE * ATT_HEADS, V_DIM)
    sconv = jnp.transpose(state_conv[0], (1, 0, 2))
    xs_in = x_sample.reshape(bs, D_MODEL)
    z_s, xbc_s, dt_s = _inproj_a(xs_in, g_a, wz, wx, wdt, tm=bs)
    conv_s, xdt, dec, bm, cm, ydiag, xs_act = _ssd_step_pre(xbc_s, dt_s, sconv, convw, convb, dtb, alog, hexp, gexp)
    yoff, ssm_s = _ssd_step_state(xdt, dec, bm, cm, state_ssm[0])
    h1_s = _ssd_step_out(ydiag, yoff, dec, xs_act, z_s, xs_in, dskip, gnorm, wout_a)
    tables_s = _rope_tables(jnp.full((bs,), float(PAST_LEN), F32))
    k_s, v_s, q_s, gate_s = _kvq(h1_s, g_kv, g_b, wkv, wb, *tables_s, tm=bs)
    o_s = _paged(page_table, q_s, k_s, v_s, cache_kt, cache_vr, lq1, lk1, lq2, lk2, subln)
    y_s = _attn_out(o_s, gate_s, h1_s, wout_b, g_f, tm=bs)

    return (y_p.reshape(bp, lp, D_MODEL), y_s.reshape(bs, 1, D_MODEL),
            k_p, v_p.reshape(bp, lp, ATT_HEADS, V_DIM),
            conv_p[None], ssm_p[None],
            k_s.reshape(bs, 1, ATT_HEADS, 2, HEAD_DIM), v_s.reshape(bs, 1, ATT_HEADS, V_DIM),
            jnp.transpose(conv_s, (1, 0, 2))[None], ssm_s[None])
```

```python
import functools
import math

import jax
import jax.numpy as jnp
from jax import lax
from jax.experimental import pallas as pl
from jax.experimental.pallas import tpu as pltpu

F32 = jnp.float32
BF16 = jnp.bfloat16
HIGHEST = lax.Precision.HIGHEST

D_MODEL = 1024
PAST_LEN = 8192
PAGE_SIZE = 128
N_PAGES = PAST_LEN // PAGE_SIZE
D_INNER = 2048
SSM_HEAD_DIM = 64
SSM_HEADS = 32
N_GROUPS = 4
D_STATE = 128
D_CONV = 4
CONV_DIM = D_INNER + 2 * N_GROUPS * D_STATE
CHUNK = 128
ATT_HEADS = 8
HEAD_DIM = 64
V_DIM = 128
K_DIM = 1024
VAL_WIDTH = 1024
ROT_DIM = 16
ROPE_THETA = 500000.0
EPS = 1e-6
NEG_INF = -1e30
LAMBDA_INIT = 0.8 - 0.6 * math.exp(-0.3 * 1)

FLASH_TK = 256
VT_ROWS = V_DIM + 16
Q_SCALE_LOG2 = HEAD_DIM ** -0.5 * math.log2(math.e)

LANES = 128
VMEM_LIMIT = 56 * 1024 * 1024

NT_DIMS = (((1,), (1,)), ((), ()))
TN_DIMS = (((0,), (0,)), ((), ()))


def _params(*sem):
    return pltpu.CompilerParams(dimension_semantics=sem, vmem_limit_bytes=VMEM_LIMIT)


def _silu(x):
    return x * (1.0 / (1.0 + jnp.exp(-x)))


def _softplus(x):
    return jnp.maximum(x, 0.0) + jnp.log1p(jnp.exp(-jnp.abs(x)))


def _rms(x):
    return x * lax.rsqrt(jnp.mean(x * x, axis=-1, keepdims=True) + EPS)


def _dot(a, b):
    return jnp.dot(a, b, preferred_element_type=F32)


def _full(shape):
    return pl.BlockSpec(shape, lambda *_: (0,) * len(shape))


def _inproj_a_kernel(x_ref, g_ref, wz_ref, wx_ref, wdt_ref, z_ref, xbc_ref, dt_ref):
    xn = (_rms(x_ref[...]) * g_ref[...]).astype(BF16)
    z_ref[...] = _dot(xn, wz_ref[...])
    xbc_ref[...] = _dot(xn, wx_ref[...])
    dt_ref[...] = _dot(xn, wdt_ref[...])


def _inproj_a(x, g, wz, wx, wdt, tm):
    m = x.shape[0]
    row = lambda n: pl.BlockSpec((tm, n), lambda i: (i, 0))
    return pl.pallas_call(
        _inproj_a_kernel,
        grid=(m // tm,),
        in_specs=[row(D_MODEL), _full((1, D_MODEL)), _full(wz.shape), _full(wx.shape), _full(wdt.shape)],
        out_specs=[row(D_INNER), row(CONV_DIM), row(LANES)],
        out_shape=[jax.ShapeDtypeStruct((m, D_INNER), F32),
                   jax.ShapeDtypeStruct((m, CONV_DIM), F32),
                   jax.ShapeDtypeStruct((m, LANES), F32)],
        compiler_params=_params("arbitrary"),
        name="inproj_a",
    )(x, g, wz, wx, wdt)


def _gate_norm_outproj(y, z, gnorm, wout, xres):
    yg = y * _silu(z)
    gw = D_INNER // N_GROUPS
    parts = [_rms(yg[:, g * gw:(g + 1) * gw]) for g in range(N_GROUPS)]
    yn = (jnp.concatenate(parts, axis=1) * gnorm).astype(BF16)
    return xres + _dot(yn, wout)


def _ssd_prompt_kernel(xbc_ref, dt_ref, z_ref, x_ref, convw_ref, convb_ref, dtb_ref, alog_ref,
                       dskip_ref, gnorm_ref, wout_ref,
                       h_ref, conv_out_ref, ssm_out_ref,
                       xpad_ref, act_ref, y_ref, st_ref):
    c = pl.program_id(1)
    last = pl.num_programs(1) - 1
    q = CHUNK

    @pl.when(c == 0)
    def _():
        xpad_ref[0:8, :] = jnp.zeros((8, CONV_DIM), F32)
        st_ref[...] = jnp.zeros_like(st_ref)

    xbc = xbc_ref[0]
    xpad_ref[8:8 + q, :] = xbc
    conv = convb_ref[...] + xbc * convw_ref[3:4, :]
    xpad = xpad_ref[...]
    for k in range(1, D_CONV):
        conv = conv + pltpu.roll(xpad, k, 0)[8:, :] * convw_ref[D_CONV - 1 - k:D_CONV - k, :]
    act_ref[...] = _silu(conv)
    xpad_ref[0:8, :] = xpad_ref[q:q + 8, :]

    @pl.when(c == last)
    def _():
        conv_out_ref[0] = xpad_ref[5:8, :]

    dt = _softplus(dt_ref[0] + dtb_ref[...])
    a = -jnp.exp(alog_ref[...])
    row_i = lax.broadcasted_iota(jnp.int32, (q, q), 0)
    col_i = lax.broadcasted_iota(jnp.int32, (q, q), 1)
    causal = row_i >= col_i
    a_cs = jnp.dot(causal.astype(F32), dt * a, precision=HIGHEST, preferred_element_type=F32)
    a_cs_t = a_cs.T
    dt_t = dt.T
    w_t = dt_t * jnp.exp(a_cs_t[:, q - 1:q] - a_cs_t)
    e_cs = jnp.exp(a_cs)
    first_half = col_i < SSM_HEAD_DIM

    for g in range(N_GROUPS):
        bm = act_ref[:, D_INNER + g * D_STATE:D_INNER + (g + 1) * D_STATE]
        cm = act_ref[:, D_INNER + (N_GROUPS + g) * D_STATE:D_INNER + (N_GROUPS + g + 1) * D_STATE]
        bm_b = bm.astype(BF16)
        cm_b = cm.astype(BF16)
        cb = lax.dot_general(cm_b, bm_b, NT_DIMS, preferred_element_type=F32)
        bm_t = bm.T
        for jj in range(SSM_HEADS // N_GROUPS // 2):
            j = g * (SSM_HEADS // N_GROUPS // 2) + jj
            xs = act_ref[:, j * LANES:(j + 1) * LANES]
            xs_b = xs.astype(BF16)
            zero = jnp.zeros_like(xs_b)
            xbd = jnp.concatenate([jnp.where(first_half, xs_b, zero),
                                   jnp.where(first_half, zero, xs_b)], axis=0)
            m_parts, w_parts = [], []
            for h in (2 * j, 2 * j + 1):
                seg = a_cs[:, h:h + 1] - a_cs_t[h:h + 1, :]
                decay = jnp.exp(jnp.where(causal, seg, -jnp.inf))
                m_parts.append((cb * decay * dt_t[h:h + 1, :]).astype(BF16))
                w_parts.append((bm_t * w_t[h:h + 1, :]).astype(BF16))
            y_diag = _dot(jnp.concatenate(m_parts, axis=1), xbd)
            upd_t = _dot(jnp.concatenate(w_parts, axis=1), xbd)
            st = st_ref[j]
            e_sel = jnp.where(first_half, e_cs[:, 2 * j:2 * j + 1], e_cs[:, 2 * j + 1:2 * j + 2])
            y_off = _dot(cm_b, st.astype(BF16)) * e_sel
            st_new = st * e_sel[q - 1:q, :] + upd_t
            st_ref[j] = st_new
            y_ref[:, j * LANES:(j + 1) * LANES] = y_diag + y_off + dskip_ref[:, j * LANES:(j + 1) * LANES] * xs

            @pl.when(c == last)
            def _():
                ssm_out_ref[0, 2 * j:2 * j + 2] = st_new.T.reshape(2, SSM_HEAD_DIM, D_STATE)

    h_ref[0] = _gate_norm_outproj(y_ref[...], z_ref[0], gnorm_ref[...], wout_ref[...], x_ref[0])


def _ssd_prompt(xbc, dt, z, x, convw, convb, dtb, alog, dskip, gnorm, wout):
    b, l, _ = x.shape
    nc = l // CHUNK
    tok = lambda n: pl.BlockSpec((1, CHUNK, n), lambda i, c: (i, c, 0))
    return pl.pallas_call(
        _ssd_prompt_kernel,
        grid=(b, nc),
        in_specs=[tok(CONV_DIM), tok(LANES), tok(D_INNER), tok(D_MODEL),
                  _full(convw.shape), _full(convb.shape), _full(dtb.shape), _full(alog.shape),
                  _full(dskip.shape), _full(gnorm.shape), _full(wout.shape)],
        out_specs=[tok(D_MODEL),
                   pl.BlockSpec((1, D_CONV - 1, CONV_DIM), lambda i, c: (i, 0, 0)),
                   pl.BlockSpec((1, SSM_HEADS, SSM_HEAD_DIM, D_STATE), lambda i, c: (i, 0, 0, 0))],
        out_shape=[jax.ShapeDtypeStruct((b, l, D_MODEL), F32),
                   jax.ShapeDtypeStruct((b, D_CONV - 1, CONV_DIM), F32),
                   jax.ShapeDtypeStruct((b, SSM_HEADS, SSM_HEAD_DIM, D_STATE), F32)],
        scratch_shapes=[pltpu.VMEM((CHUNK + 8, CONV_DIM), F32),
                        pltpu.VMEM((CHUNK, CONV_DIM), F32),
                        pltpu.VMEM((CHUNK, D_INNER), F32),
                        pltpu.VMEM((SSM_HEADS // 2, D_STATE, LANES), F32)],
        compiler_params=_params("arbitrary", "arbitrary"),
        name="ssd_prompt",
    )(xbc, dt, z, x, convw, convb, dtb, alog, dskip, gnorm, wout)


def _ssd_step_pre_kernel(xbc_ref, dt_ref, sconv_ref, convw_ref, convb_ref, dtb_ref, alog_ref,
                         hexp_ref, gexp_ref,
                         conv_out_ref, xdt_ref, dec_ref, bm_ref, cm_ref, ydiag_ref, xs_ref):
    xbc = xbc_ref[...]
    conv = convb_ref[...] + xbc * convw_ref[3:4, :]
    for k in range(D_CONV - 1):
        conv = conv + sconv_ref[k] * convw_ref[k:k + 1, :]
    conv_out_ref[0] = sconv_ref[1]
    conv_out_ref[1] = sconv_ref[2]
    conv_out_ref[2] = xbc
    act = _silu(conv)
    xs = act[:, :D_INNER]
    bm = act[:, D_INNER:D_INNER + N_GROUPS * D_STATE]
    cm = act[:, D_INNER + N_GROUPS * D_STATE:]
    dt = _softplus(dt_ref[...] + dtb_ref[...])
    dec = jnp.exp(dt * -jnp.exp(alog_ref[...]))
    hexp = hexp_ref[...]
    dt_x = jnp.dot(dt, hexp, precision=HIGHEST, preferred_element_type=F32)
    dec_x = jnp.dot(dec, hexp, precision=HIGHEST, preferred_element_type=F32)
    prod = bm * cm
    cb = jnp.concatenate([jnp.sum(prod[:, g * D_STATE:(g + 1) * D_STATE], axis=1, keepdims=True)
                          for g in range(N_GROUPS)], axis=1)
    cb_x = jnp.dot(cb, gexp_ref[...], precision=HIGHEST, preferred_element_type=F32)
    xdt = xs * dt_x
    xdt_ref[...] = xdt
    dec_ref[...] = dec_x
    bm_ref[...] = bm
    cm_ref[...] = cm
    ydiag_ref[...] = cb_x * xdt
    xs_ref[...] = xs


def _ssd_step_pre(xbc, dt, sconv, convw, convb, dtb, alog, hexp, gexp):
    m = xbc.shape[0]
    gs = N_GROUPS * D_STATE
    return pl.pallas_call(
        _ssd_step_pre_kernel,
        out_shape=[jax.ShapeDtypeStruct((D_CONV - 1, m, CONV_DIM), F32),
                   jax.ShapeDtypeStruct((m, D_INNER), F32),
                   jax.ShapeDtypeStruct((m, D_INNER), F32),
                   jax.ShapeDtypeStruct((m, gs), F32),
                   jax.ShapeDtypeStruct((m, gs), F32),
                   jax.ShapeDtypeStruct((m, D_INNER), F32),
                   jax.ShapeDtypeStruct((m, D_INNER), F32)],
        compiler_params=pltpu.CompilerParams(vmem_limit_bytes=VMEM_LIMIT),
        name="ssd_step_pre",
    )(xbc, dt, sconv, convw, convb, dtb, alog, hexp, gexp)


def _ssd_step_state_kernel(xdt_ref, dec_ref, bm_ref, cm_ref, st_ref, yoff_ref, st_out_ref):
    gw = D_INNER // N_GROUPS
    hg = SSM_HEADS // N_GROUPS
    sub = lax.broadcasted_iota(jnp.int32, (8, gw), 0)
    sub_n = lax.broadcasted_iota(jnp.int32, (8, D_STATE), 0)
    ones_row = jnp.where(sub_n == 0, 1.0, 0.0).astype(F32)
    for g in range(N_GROUPS):
        xdt8 = jnp.where(sub == 0, xdt_ref[0, :, g * gw:(g + 1) * gw], 0.0)
        dec8 = jnp.where(sub == 0, dec_ref[0, :, g * gw:(g + 1) * gw], 0.0)
        bm8 = jnp.where(sub_n == 0, bm_ref[0, :, g * D_STATE:(g + 1) * D_STATE], 0.0)
        cm8 = jnp.broadcast_to(cm_ref[0, :, g * D_STATE:(g + 1) * D_STATE], (8, D_STATE)).astype(BF16)
        st = st_ref[0, g * hg:(g + 1) * hg].reshape(gw, D_STATE)
        y8 = lax.dot_general(cm8, st.astype(BF16), NT_DIMS, preferred_element_type=F32)
        yoff_ref[0, :, g * gw:(g + 1) * gw] = y8[0:1, :]
        dec_col = lax.dot_general(dec8, ones_row, TN_DIMS, precision=HIGHEST, preferred_element_type=F32)
        upd = lax.dot_general(xdt8, bm8, TN_DIMS, precision=HIGHEST, preferred_element_type=F32)
        st_out_ref[0, g * hg:(g + 1) * hg] = (st * dec_col + upd).reshape(hg, SSM_HEAD_DIM, D_STATE)


def _ssd_step_state(xdt, dec, bm, cm, st):
    m = xdt.shape[0]
    gs = N_GROUPS * D_STATE
    row = lambda n: pl.BlockSpec((1, 1, n), lambda i: (i, 0, 0))
    st_spec = pl.BlockSpec((1, SSM_HEADS, SSM_HEAD_DIM, D_STATE), lambda i: (i, 0, 0, 0))
    yoff, st_out = pl.pallas_call(
        _ssd_step_state_kernel,
        grid=(m,),
        in_specs=[row(D_INNER), row(D_INNER), row(gs), row(gs), st_spec],
        out_specs=[row(D_INNER), st_spec],
        out_shape=[jax.ShapeDtypeStruct((m, 1, D_INNER), F32),
                   jax.ShapeDtypeStruct(st.shape, F32)],
        compiler_params=_params("arbitrary"),
        name="ssd_step_state",
    )(xdt.reshape(m, 1, D_INNER), dec.reshape(m, 1, D_INNER), bm.reshape(m, 1, gs), cm.reshape(m, 1, gs), st)
    return yoff.reshape(m, D_INNER), st_out


def _ssd_step_out_kernel(ydiag_ref, yoff_ref, dec_ref, xs_ref, z_ref, x_ref, dskip_ref, gnorm_ref, wout_ref,
                         h_ref):
    y = ydiag_ref[...] + yoff_ref[...] * dec_ref[...] + dskip_ref[...] * xs_ref[...]
    h_ref[...] = _gate_norm_outproj(y, z_ref[...], gnorm_ref[...], wout_ref[...], x_ref[...])


def _ssd_step_out(ydiag, yoff, dec, xs, z, x, dskip, gnorm, wout):
    return pl.pallas_call(
        _ssd_step_out_kernel,
        out_shape=jax.ShapeDtypeStruct(x.shape, F32),
        compiler_params=pltpu.CompilerParams(vmem_limit_bytes=VMEM_LIMIT),
        name="ssd_step_out",
    )(ydiag, yoff, dec, xs, z, x, dskip, gnorm, wout)


def _rope(t, cos, sin_up, sin_dn):
    cols = []
    for c in range(t.shape[1] // LANES):
        tc = t[:, c * LANES:(c + 1) * LANES]
        cols.append(tc * cos + pltpu.roll(tc, ROT_DIM // 2, 1) * sin_up
                    + pltpu.roll(tc, LANES - ROT_DIM // 2, 1) * sin_dn)
    return jnp.concatenate(cols, axis=1)


def _kvq_kernel(h_ref, gkv_ref, gb_ref, wkv_ref, wb_ref, cos_ref, sup_ref, sdn_ref, *out_refs, for_flash):
    xh = _rms(h_ref[...])
    kv = _dot((xh * gkv_ref[...]).astype(BF16), wkv_ref[...])
    qg = _dot((xh * gb_ref[...]).astype(BF16), wb_ref[...])
    cos, sup, sdn = cos_ref[...], sup_ref[...], sdn_ref[...]
    k = _rope(kv[:, :K_DIM], cos, sup, sdn)
    v = kv[:, K_DIM:]
    q = _rope(qg[:, :K_DIM], cos, sup, sdn)
    if for_flash:
        kt_ref, v_ref, kb_ref, vt_ref, qt_ref, gate_ref = out_refs
        kt_ref[0] = k.T
        kb_ref[...] = k.astype(BF16)
        vt = v.T.astype(BF16)
        for h in range(ATT_HEADS):
            vt_ref[0, 0, h, :V_DIM, :] = vt[h * V_DIM:(h + 1) * V_DIM]
            vt_ref[0, 0, h, V_DIM:, :] = jnp.ones((VT_ROWS - V_DIM, vt.shape[1]), BF16)
        qt_ref[0] = (q * Q_SCALE_LOG2).T.astype(BF16)
    else:
        k_ref, v_ref, q_ref, gate_ref = out_refs
        k_ref[...] = k
        q_ref[...] = q
    v_ref[...] = v
    gate_ref[...] = qg[:, K_DIM:]


def _kvq_in_specs(tm, wkv, wb, table_blocks):
    table = pl.BlockSpec((tm, LANES), lambda i: (i % table_blocks, 0))
    return [pl.BlockSpec((tm, D_MODEL), lambda i: (i, 0)), _full((1, D_MODEL)), _full((1, D_MODEL)),
            _full(wkv.shape), _full(wb.shape), table, table, table]


def _kvq(h, gkv, gb, wkv, wb, cos, sup, sdn, tm):
    m = h.shape[0]
    row = pl.BlockSpec((tm, D_MODEL), lambda i: (i, 0))
    out = jax.ShapeDtypeStruct((m, D_MODEL), F32)
    return pl.pallas_call(
        functools.partial(_kvq_kernel, for_flash=False),
        grid=(m // tm,),
        in_specs=_kvq_in_specs(tm, wkv, wb, cos.shape[0] // tm),
        out_specs=[row] * 4,
        out_shape=[out] * 4,
        compiler_params=_params("arbitrary"),
        name="kvq_proj",
    )(h, gkv, gb, wkv, wb, cos, sup, sdn)


def _kvq_flash(h, gkv, gb, wkv, wb, cos, sup, sdn, b, l):
    tm = FLASH_TK
    nb = l // tm
    row = pl.BlockSpec((tm, D_MODEL), lambda i: (i, 0))
    feat = pl.BlockSpec((1, D_MODEL, tm), lambda i: (i // nb, 0, i % nb))
    return pl.pallas_call(
        functools.partial(_kvq_kernel, for_flash=True),
        grid=(b * nb,),
        in_specs=_kvq_in_specs(tm, wkv, wb, nb),
        out_specs=[feat, row, row,
                   pl.BlockSpec((1, 1, ATT_HEADS, VT_ROWS, tm), lambda i: (i // nb, i % nb, 0, 0, 0)), feat, row],
        out_shape=[jax.ShapeDtypeStruct((b, K_DIM, l), F32),
                   jax.ShapeDtypeStruct((b * l, VAL_WIDTH), F32),
                   jax.ShapeDtypeStruct((b * l, K_DIM), BF16),
                   jax.ShapeDtypeStruct((b, nb, ATT_HEADS, VT_ROWS, tm), BF16),
                   jax.ShapeDtypeStruct((b, K_DIM, l), BF16),
                   jax.ShapeDtypeStruct((b * l, VAL_WIDTH), F32)],
        compiler_params=_params("arbitrary"),
        name="kvq_proj_flash",
    )(h, gkv, gb, wkv, wb, cos, sup, sdn)


def _rope_tables(pos):
    inv_freq = ROPE_THETA ** (-jnp.arange(0, ROT_DIM, 2, dtype=F32) / ROT_DIM)
    ang = pos[:, None] * inv_freq[None, :]
    cos, sin = jnp.cos(ang), jnp.sin(ang)
    half = ROT_DIM // 2
    rows = pos.shape[0]
    pad = jnp.zeros((rows, HEAD_DIM - ROT_DIM), F32)
    zero = jnp.zeros((rows, half), F32)
    cos64 = jnp.concatenate([cos, cos, pad + 1.0], axis=1)
    sup64 = jnp.concatenate([zero, sin, pad], axis=1)
    sdn64 = jnp.concatenate([-sin, zero, pad], axis=1)
    return tuple(jnp.tile(t, (1, LANES // HEAD_DIM)) for t in (cos64, sup64, sdn64))


def _lambda(lq1_ref, lk1_ref, lq2_ref, lk2_ref):
    s1 = jnp.sum(lq1_ref[...] * lk1_ref[...], axis=1, keepdims=True)
    s2 = jnp.sum(lq2_ref[...] * lk2_ref[...], axis=1, keepdims=True)
    return jnp.exp(s1) - jnp.exp(s2) + LAMBDA_INIT


def _sub_norm(o, subln):
    return _rms(o) * subln * (1.0 - LAMBDA_INIT)


def _flash_kernel(qt_ref, k_ref, vt_ref, lq1_ref, lk1_ref, lq2_ref, lk2_ref, subln_ref, o_ref,
                  acc_ref, sa_ref, sb_ref, *, tq, tk):
    qi = pl.program_id(2)
    n = 2 * tq
    qt = qt_ref[0]
    feat = lax.broadcasted_iota(jnp.int32, (LANES, tq), 0)
    zero = jnp.zeros_like(qt)
    qqt = jnp.concatenate([jnp.where(feat < HEAD_DIM, qt, zero), jnp.where(feat < HEAD_DIM, zero, qt)], axis=1)
    acc_ref[...] = jnp.zeros_like(acc_ref)

    def logits(kb, lanes=slice(None)):
        start = pl.multiple_of(kb * tk, tk)
        return _dot(k_ref[0, pl.ds(start, tk), :], qqt[:, lanes])

    def softmax_pv(st, m, kb, lanes=slice(None)):
        m_new = jnp.maximum(m, jnp.max(st, axis=0, keepdims=True))
        alpha = jnp.exp2(m - m_new)
        p = jnp.exp2(st - m_new).astype(BF16)
        acc_ref[:, lanes] = acc_ref[:, lanes] * alpha + _dot(vt_ref[0, kb, 0], p)
        return m_new

    sa_ref[...] = logits(0)

    def pair(i, m):
        kb = 2 * i
        sb_ref[...] = logits(kb + 1)
        m = softmax_pv(sa_ref[...], m, kb)
        sa_ref[...] = logits(kb + 2)
        return softmax_pv(sb_ref[...], m, kb + 1)

    m = lax.fori_loop(0, qi, pair, jnp.full((1, n), NEG_INF, F32))
    kb = 2 * qi
    late = (slice(tk, tq), slice(tq + tk, n))
    tri = lax.broadcasted_iota(jnp.int32, (tk, tk), 0) <= lax.broadcasted_iota(jnp.int32, (tk, tk), 1)
    late_logits = [jnp.where(tri, logits(kb + 1, lanes), NEG_INF) for lanes in late]
    key_i = lax.broadcasted_iota(jnp.int32, (tk, n), 0)
    qry_i = lax.broadcasted_iota(jnp.int32, (tk, n), 1) % tq
    m = softmax_pv(jnp.where(key_i <= qry_i, sa_ref[...], NEG_INF), m, kb)
    for lanes, st in zip(late, late_logits):
        softmax_pv(st, m[:, lanes], kb + 1, lanes)

    on = acc_ref[:V_DIM, :] * (1.0 / acc_ref[V_DIM:V_DIM + 1, :])
    ot = on[:, :tq] - _lambda(lq1_ref, lk1_ref, lq2_ref, lk2_ref) * on[:, tq:]
    ot = ot * lax.rsqrt(jnp.mean(ot * ot, axis=0, keepdims=True) + EPS) * subln_ref[...] * (1.0 - LAMBDA_INIT)
    o_ref[0] = ot.T


def _flash(qt, kb, vt, lq1, lk1, lq2, lk2, subln_col, tq):
    b, _, l = qt.shape
    tk = FLASH_TK
    assert tq == 2 * tk
    small = _full((1, HEAD_DIM))
    return pl.pallas_call(
        functools.partial(_flash_kernel, tq=tq, tk=tk),
        grid=(b, ATT_HEADS, l // tq),
        in_specs=[pl.BlockSpec((1, LANES, tq), lambda i, h, t: (i, h, t)),
                  pl.BlockSpec((1, l, LANES), lambda i, h, t: (i, 0, h)),
                  pl.BlockSpec((1, l // tk, 1, VT_ROWS, tk), lambda i, h, t: (i, 0, h, 0, 0)),
                  small, small, small, small, _full((V_DIM, 1))],
        out_specs=pl.BlockSpec((1, tq, LANES), lambda i, h, t: (i, t, h)),
        out_shape=jax.ShapeDtypeStruct((b, l, VAL_WIDTH), F32),
        scratch_shapes=[pltpu.VMEM((VT_ROWS, 2 * tq), F32),
                        pltpu.VMEM((tk, 2 * tq), F32),
                        pltpu.VMEM((tk, 2 * tq), F32)],
        compiler_params=_params("arbitrary", "arbitrary", "arbitrary"),
        name="flash_diff",
    )(qt, kb, vt, lq1, lk1, lq2, lk2, subln_col)


PAGES_PER_STEP = 16
N_MAPS = 2 * ATT_HEADS


def _paged_kernel(pt_ref, q_ref, kn_ref, vn_ref, lq1_ref, lk1_ref, lq2_ref, lk2_ref, subln_ref, *rest):
    kt_refs = rest[:PAGES_PER_STEP]
    v_refs = rest[PAGES_PER_STEP:2 * PAGES_PER_STEP]
    o_ref = rest[2 * PAGES_PER_STEP]
    qrow_ref, m_ref, l_ref, acc_ref = rest[2 * PAGES_PER_STEP + 1:]
    step = pl.program_id(1)
    head_of_map = lax.broadcasted_iota(jnp.int32, (N_MAPS, PAGE_SIZE), 0) % ATT_HEADS

    @pl.when(step == 0)
    def _():
        map_i = lax.broadcasted_iota(jnp.int32, (N_MAPS, K_DIM), 0)
        seg = 2 * (map_i % ATT_HEADS) + map_i // ATT_HEADS
        lane_seg = lax.broadcasted_iota(jnp.int32, (N_MAPS, K_DIM), 1) // HEAD_DIM
        qs = q_ref[0] * (HEAD_DIM ** -0.5)
        qrow = jnp.where(lane_seg == seg, jnp.broadcast_to(qs, (N_MAPS, K_DIM)), 0.0)
        qrow_ref[...] = qrow.astype(BF16)
        m_ref[...] = jnp.sum(qrow * kn_ref[0], axis=1, keepdims=True)
        l_ref[...] = jnp.ones_like(l_ref)
        acc_ref[...] = jnp.concatenate([vn_ref[0], vn_ref[0]], axis=0)

    qrow = qrow_ref[...]
    s = jnp.concatenate([_dot(qrow, kt_refs[i][0].astype(BF16)) for i in range(PAGES_PER_STEP)], axis=1)
    m_old = m_ref[...]
    m_new = jnp.maximum(m_old, jnp.max(s, axis=1, keepdims=True))
    alpha = jnp.exp(m_old - m_new)
    p = jnp.exp(s - m_new)
    l_ref[...] = alpha * l_ref[...] + jnp.sum(p, axis=1, keepdims=True)
    m_ref[...] = m_new
    pv = jnp.zeros((N_MAPS, V_DIM), F32)
    for i in range(PAGES_PER_STEP):
        p_i = p[:, i * PAGE_SIZE:(i + 1) * PAGE_SIZE]
        p_heads = jnp.concatenate([jnp.where(head_of_map == h, p_i, 0.0) for h in range(ATT_HEADS)], axis=1)
        v_heads = jnp.concatenate([v_refs[i][0, pl.ds(h, PAGE_SIZE, stride=ATT_HEADS), :]
                                   for h in range(ATT_HEADS)], axis=0)
        pv = pv + _dot(p_heads.astype(BF16), v_heads.astype(BF16))
    acc_ref[...] = alpha * acc_ref[...] + pv

    @pl.when(step == pl.num_programs(1) - 1)
    def _():
        on = acc_ref[...] / l_ref[...]
        o = on[:ATT_HEADS] - _lambda(lq1_ref, lk1_ref, lq2_ref, lk2_ref) * on[ATT_HEADS:]
        o_ref[0] = _sub_norm(o, subln_ref[...])


def _paged(page_table, q, kn, vn, cache_kt, cache_v, lq1, lk1, lq2, lk2, subln):
    m = q.shape[0]
    n_steps = N_PAGES // PAGES_PER_STEP
    row = pl.BlockSpec((1, 1, D_MODEL), lambda i, s, pt: (i, 0, 0))
    heads = pl.BlockSpec((1, ATT_HEADS, V_DIM), lambda i, s, pt: (i, 0, 0))
    small = pl.BlockSpec((1, HEAD_DIM), lambda i, s, pt: (0, 0))

    def page(r):
        return pl.BlockSpec((1, K_DIM, PAGE_SIZE),
                            lambda i, s, pt: (pt[i * N_PAGES + s * PAGES_PER_STEP + r], 0, 0))

    pages = [page(r) for r in range(PAGES_PER_STEP)]
    grid_spec = pltpu.PrefetchScalarGridSpec(
        num_scalar_prefetch=1,
        grid=(m, n_steps),
        in_specs=[row, row, heads, small, small, small, small,
                  pl.BlockSpec((1, V_DIM), lambda i, s, pt: (0, 0))] + pages + pages,
        out_specs=heads,
        scratch_shapes=[pltpu.VMEM((N_MAPS, K_DIM), BF16),
                        pltpu.VMEM((N_MAPS, 1), F32),
                        pltpu.VMEM((N_MAPS, 1), F32),
                        pltpu.VMEM((N_MAPS, V_DIM), F32)],
    )
    out = pl.pallas_call(
        _paged_kernel,
        grid_spec=grid_spec,
        out_shape=jax.ShapeDtypeStruct((m, ATT_HEADS, V_DIM), F32),
        compiler_params=_params("arbitrary", "arbitrary"),
        name="paged_diff",
    )(page_table.reshape(-1), q.reshape(m, 1, K_DIM), kn.reshape(m, 1, K_DIM), vn.reshape(m, ATT_HEADS, V_DIM),
      lq1, lk1, lq2, lk2, subln, *([cache_kt] * PAGES_PER_STEP), *([cache_v] * PAGES_PER_STEP))
    return out.reshape(m, VAL_WIDTH)


def _attn_out_kernel(o_ref, gate_ref, h_ref, wout_ref, gf_ref, y_ref):
    og = (o_ref[...] * _silu(gate_ref[...])).astype(BF16)
    y_ref[...] = _rms(h_ref[...] + _dot(og, wout_ref[...])) * gf_ref[...]


def _attn_out(o, gate, h, wout, gf, tm):
    m = o.shape[0]
    row = pl.BlockSpec((tm, D_MODEL), lambda i: (i, 0))
    return pl.pallas_call(
        _attn_out_kernel,
        grid=(m // tm,),
        in_specs=[row, row, row, _full(wout.shape), _full((1, D_MODEL))],
        out_specs=row,
        out_shape=jax.ShapeDtypeStruct((m, D_MODEL), F32),
        compiler_params=_params("arbitrary"),
        name="attn_out",
    )(o, gate, h, wout, gf)


def _pad_lanes(v):
    return jnp.pad(v.reshape(1, -1), ((0, 0), (0, LANES - v.shape[-1])))


def kernel(x_prompt, x_sample, cache_k, cache_v, page_table, state_conv, state_ssm, norm_a, w_in_a, conv_w, conv_b, dt_bias, a_log, d_skip, gnorm_a, w_out_a, norm_kv, w_kv, norm_b, w_in_b, lambda_q1, lambda_k1, lambda_q2, lambda_k2, subln_b, w_out_b, norm_f):
    bp, lp, _ = x_prompt.shape
    bs = x_sample.shape[0]
    mp = bp * lp

    w_a = w_in_a[0].astype(BF16)
    wz, wx = w_a[:, :D_INNER], w_a[:, D_INNER:D_INNER + CONV_DIM]
    wdt = jnp.pad(w_a[:, D_INNER + CONV_DIM:], ((0, 0), (0, LANES - SSM_HEADS)))
    wout_a = w_out_a[0].astype(BF16)
    wkv = w_kv.astype(BF16)
    wb = w_in_b[0].astype(BF16)
    wout_b = w_out_b[0].astype(BF16)
    g_a = norm_a[0].reshape(1, D_MODEL)
    convw, convb = conv_w[0], conv_b[0].reshape(1, CONV_DIM)
    dtb, alog = _pad_lanes(dt_bias[0]), _pad_lanes(a_log[0])
    dskip = jnp.repeat(d_skip[0], SSM_HEAD_DIM).reshape(1, D_INNER)
    gnorm = gnorm_a[0].reshape(1, D_INNER)
    g_kv, g_b, g_f = norm_kv.reshape(1, D_MODEL), norm_b[0].reshape(1, D_MODEL), norm_f.reshape(1, D_MODEL)
    lq1, lk1, lq2, lk2 = (t[0].reshape(1, HEAD_DIM) for t in (lambda_q1, lambda_k1, lambda_q2, lambda_k2))
    subln = subln_b[0].reshape(1, V_DIM)
    head_of_lane = jnp.arange(D_INNER) // SSM_HEAD_DIM
    hexp = (jnp.arange(LANES)[:, None] == head_of_lane[None, :]).astype(F32)
    gexp = (jnp.arange(N_GROUPS)[:, None] == (head_of_lane // (SSM_HEADS // N_GROUPS))[None, :]).astype(F32)

    xp = x_prompt.reshape(mp, D_MODEL)
    z, xbc, dt = _inproj_a(xp, g_a, wz, wx, wdt, tm=256)
    h1, conv_p, ssm_p = _ssd_prompt(xbc.reshape(bp, lp, CONV_DIM), dt.reshape(bp, lp, LANES),
                                    z.reshape(bp, lp, D_INNER), x_prompt,
                                    convw, convb, dtb, alog, dskip, gnorm, wout_a)
    h1 = h1.reshape(mp, D_MODEL)
    tables_p = _rope_tables(jnp.arange(lp, dtype=F32))
    kt_p, v_p, kb_p, vt_p, qt_p, gate_p = _kvq_flash(h1, g_kv, g_b, wkv, wb, *tables_p, b=bp, l=lp)
    o_p = _flash(qt_p, kb_p.reshape(bp, lp, K_DIM), vt_p, lq1, lk1, lq2, lk2, subln.reshape(V_DIM, 1), tq=512)
    y_p = _attn_out(o_p.reshape(mp, VAL_WIDTH), gate_p, h1, wout_b, g_f, tm=256)
    k_p = jnp.transpose(kt_p.reshape(bp, ATT_HEADS, 2, HEAD_DIM, lp), (0, 4, 1, 2, 3))

    n_pool = cache_k.shape[0]
    cache_kt = jnp.transpose(cache_k, (0, 2, 3, 4, 1)).reshape(n_pool, K_DIM, PAGE_SIZE)
    cache_vr = cache_v.reshape(n_pool, PAGE_SIZE * ATT_HEADS, V_DIM)
    sconv = jnp.transpose(state_conv[0], (1, 0, 2))
    xs_in = x_sample.reshape(bs, D_MODEL)
    z_s, xbc_s, dt_s = _inproj_a(xs_in, g_a, wz, wx, wdt, tm=bs)
    conv_s, xdt, dec, bm, cm, ydiag, xs_act = _ssd_step_pre(xbc_s, dt_s, sconv, convw, convb, dtb, alog, hexp, gexp)
    yoff, ssm_s = _ssd_step_state(xdt, dec, bm, cm, state_ssm[0])
    h1_s = _ssd_step_out(ydiag, yoff, dec, xs_act, z_s, xs_in, dskip, gnorm, wout_a)
    tables_s = _rope_tables(jnp.full((bs,), float(PAST_LEN), F32))
    k_s, v_s, q_s, gate_s = _kvq(h1_s, g_kv, g_b, wkv, wb, *tables_s, tm=bs)
    o_s = _paged(page_table, q_s, k_s, v_s, cache_kt, cache_vr, lq1, lk1, lq2, lk2, subln)
    y_s = _attn_out(o_s, gate_s, h1_s, wout_b, g_f, tm=bs)

    return (y_p.reshape(bp, lp, D_MODEL), y_s.reshape(bs, 1, D_MODEL),
            k_p, v_p.reshape(bp, lp, ATT_HEADS, V_DIM),
            conv_p[None], ssm_p[None],
            k_s.reshape(bs, 1, ATT_HEADS, 2, HEAD_DIM), v_s.reshape(bs, 1, ATT_HEADS, V_DIM),
            jnp.transpose(conv_s, (1, 0, 2))[None], ssm_s[None])
```

```python
import functools
import math

import jax
import jax.numpy as jnp
from jax import lax
from jax.experimental import pallas as pl
from jax.experimental.pallas import tpu as pltpu

F32 = jnp.float32
BF16 = jnp.bfloat16
HIGHEST = lax.Precision.HIGHEST

D_MODEL = 1024
PAST_LEN = 8192
PAGE_SIZE = 128
N_PAGES = PAST_LEN // PAGE_SIZE
D_INNER = 2048
SSM_HEAD_DIM = 64
SSM_HEADS = 32
N_GROUPS = 4
D_STATE = 128
D_CONV = 4
CONV_DIM = D_INNER + 2 * N_GROUPS * D_STATE
CHUNK = 128
ATT_HEADS = 8
HEAD_DIM = 64
V_DIM = 128
K_DIM = 1024
VAL_WIDTH = 1024
ROT_DIM = 16
ROPE_THETA = 500000.0
EPS = 1e-6
NEG_INF = -1e30
LAMBDA_INIT = 0.8 - 0.6 * math.exp(-0.3 * 1)

FLASH_TK = 256
VT_ROWS = V_DIM + 16
Q_SCALE_LOG2 = HEAD_DIM ** -0.5 * math.log2(math.e)

LANES = 128
VMEM_LIMIT = 56 * 1024 * 1024

NT_DIMS = (((1,), (1,)), ((), ()))
TN_DIMS = (((0,), (0,)), ((), ()))


def _params(*sem):
    return pltpu.CompilerParams(dimension_semantics=sem, vmem_limit_bytes=VMEM_LIMIT)


def _silu(x):
    return x * (1.0 / (1.0 + jnp.exp(-x)))


def _softplus(x):
    return jnp.maximum(x, 0.0) + jnp.log1p(jnp.exp(-jnp.abs(x)))


def _rms(x):
    return x * lax.rsqrt(jnp.mean(x * x, axis=-1, keepdims=True) + EPS)


def _dot(a, b):
    return jnp.dot(a, b, preferred_element_type=F32)


def _full(shape):
    return pl.BlockSpec(shape, lambda *_: (0,) * len(shape))


def _inproj_a_kernel(x_ref, g_ref, wz_ref, wx_ref, wdt_ref, z_ref, xbc_ref, dt_ref):
    xn = (_rms(x_ref[...]) * g_ref[...]).astype(BF16)
    z_ref[...] = _dot(xn, wz_ref[...])
    xbc_ref[...] = _dot(xn, wx_ref[...])
    dt_ref[...] = _dot(xn, wdt_ref[...])


def _inproj_a(x, g, wz, wx, wdt, tm):
    m = x.shape[0]
    row = lambda n: pl.BlockSpec((tm, n), lambda i: (i, 0))
    return pl.pallas_call(
        _inproj_a_kernel,
        grid=(m // tm,),
        in_specs=[row(D_MODEL), _full((1, D_MODEL)), _full(wz.shape), _full(wx.shape), _full(wdt.shape)],
        out_specs=[row(D_INNER), row(CONV_DIM), row(LANES)],
        out_shape=[jax.ShapeDtypeStruct((m, D_INNER), F32),
                   jax.ShapeDtypeStruct((m, CONV_DIM), F32),
                   jax.ShapeDtypeStruct((m, LANES), F32)],
        compiler_params=_params("arbitrary"),
        name="inproj_a",
    )(x, g, wz, wx, wdt)


def _gate_norm_outproj(y, gate, gnorm, wout, xres):
    yg = y * gate
    gw = D_INNER // N_GROUPS
    parts = [_rms(yg[:, g * gw:(g + 1) * gw]) for g in range(N_GROUPS)]
    yn = (jnp.concatenate(parts, axis=1) * gnorm).astype(BF16)
    return xres + _dot(yn, wout)


def _project_chunk(x, g_ref, wz_ref, wx_ref, wdt_ref, proj):
    z_ref, xbc_ref, dt_ref = proj
    xn = (_rms(x) * g_ref[...]).astype(BF16)
    z_ref[...] = _dot(xn, wz_ref[...])
    xbc_ref[...] = _dot(xn, wx_ref[...])
    dt_ref[...] = _dot(xn, wdt_ref[...])


PROJ_COLS = 512


def _scan_chunk(proj, xres, convw_ref, convb_ref, dtb_ref, alog_ref, dskip_ref, gnorm_ref, wout_ref,
                xpad_ref, st_ref, work, x_ahead, g_ref, wz_ref, wx_ref, wdt_ref, proj_ahead):
    z_ref, xbc_ref, dt_ref = proj
    za_ref, xbca_ref, dta_ref = proj_ahead
    act_ref, gate_ref, y_ref = work
    q = CHUNK
    xn = (_rms(x_ahead) * g_ref[...]).astype(BF16)
    sub8 = lax.broadcasted_iota(jnp.int32, (q // 8, 8, PROJ_COLS), 1)

    for blk in range(CONV_DIM // PROJ_COLS):
        cols = slice(blk * PROJ_COLS, (blk + 1) * PROJ_COLS)
        xbca_ref[:, cols] = _dot(xn, wx_ref[:, cols])
        xbc = xbc_ref[:, cols]
        xpad_ref[8:8 + q, cols] = xbc
        conv = (convb_ref[:, cols] + xbc * convw_ref[3:4, cols]).reshape(q // 8, 8, PROJ_COLS)
        groups = xpad_ref[:, cols].reshape(q // 8 + 1, 8, PROJ_COLS)
        for k in range(1, D_CONV):
            rot = pltpu.roll(groups, k, 1)
            back_k = jnp.where(sub8 < k, rot[:q // 8], rot[1:])
            conv = conv + back_k * convw_ref[D_CONV - 1 - k:D_CONV - k, cols]
        act_ref[:, cols] = _silu(conv).reshape(q, PROJ_COLS)
        xpad_ref[0:8, cols] = xpad_ref[q:q + 8, cols]
    for blk in range(D_INNER // PROJ_COLS):
        cols = slice(blk * PROJ_COLS, (blk + 1) * PROJ_COLS)
        za_ref[:, cols] = _dot(xn, wz_ref[:, cols])
        gate_ref[:, cols] = _silu(z_ref[:, cols])
    dta_ref[...] = _dot(xn, wdt_ref[...])

    dt = _softplus(dt_ref[...] + dtb_ref[...])
    a = -jnp.exp(alog_ref[...])
    row_i = lax.broadcasted_iota(jnp.int32, (q, q), 0)
    col_i = lax.broadcasted_iota(jnp.int32, (q, q), 1)
    causal = row_i >= col_i
    a_cs = jnp.dot(causal.astype(F32), dt * a, precision=HIGHEST, preferred_element_type=F32)
    a_cs_t = a_cs.T
    dt_t = dt.T
    w_t = dt_t * jnp.exp(a_cs_t[:, q - 1:q] - a_cs_t)
    e_cs = jnp.exp(a_cs)
    first_half = col_i < SSM_HEAD_DIM

    for g in range(N_GROUPS):
        bm = act_ref[:, D_INNER + g * D_STATE:D_INNER + (g + 1) * D_STATE]
        cm = act_ref[:, D_INNER + (N_GROUPS + g) * D_STATE:D_INNER + (N_GROUPS + g + 1) * D_STATE]
        bm_b = bm.astype(BF16)
        cm_b = cm.astype(BF16)
        cb = lax.dot_general(cm_b, bm_b, NT_DIMS, preferred_element_type=F32)
        bm_t = bm.T
        for jj in range(SSM_HEADS // N_GROUPS // 2):
            j = g * (SSM_HEADS // N_GROUPS // 2) + jj
            xs = act_ref[:, j * LANES:(j + 1) * LANES]
            xs_b = xs.astype(BF16)
            zero = jnp.zeros_like(xs_b)
            xbd = jnp.concatenate([jnp.where(first_half, xs_b, zero),
                                   jnp.where(first_half, zero, xs_b)], axis=0)
            m_parts, w_parts = [], []
            for h in (2 * j, 2 * j + 1):
                seg = a_cs[:, h:h + 1] - a_cs_t[h:h + 1, :]
                decay = jnp.exp(jnp.where(causal, seg, -jnp.inf))
                m_parts.append((cb * decay * dt_t[h:h + 1, :]).astype(BF16))
                w_parts.append((bm_t * w_t[h:h + 1, :]).astype(BF16))
            y_diag = _dot(jnp.concatenate(m_parts, axis=1), xbd)
            upd_t = _dot(jnp.concatenate(w_parts, axis=1), xbd)
            st = st_ref[j]
            e_sel = jnp.where(first_half, e_cs[:, 2 * j:2 * j + 1], e_cs[:, 2 * j + 1:2 * j + 2])
            y_off = _dot(cm_b, st.astype(BF16)) * e_sel
            st_ref[j] = st * e_sel[q - 1:q, :] + upd_t
            y_ref[:, j * LANES:(j + 1) * LANES] = y_diag + y_off + dskip_ref[:, j * LANES:(j + 1) * LANES] * xs

    return _gate_norm_outproj(y_ref[...], gate_ref[...], gnorm_ref[...], wout_ref[...], xres)


def _ssd_prompt_kernel(x2_ref, xnext_ref, g_ref, wz_ref, wx_ref, wdt_ref, convw_ref, convb_ref, dtb_ref, alog_ref,
                       dskip_ref, gnorm_ref, wout_ref,
                       h_ref, conv_out_ref, ssm_out_ref,
                       z0_ref, xbc0_ref, dt0_ref, z1_ref, xbc1_ref, dt1_ref,
                       act0_ref, gate0_ref, y0_ref, act1_ref, gate1_ref, y1_ref, xpad_ref, st_ref,
                       *, steps_per_seq):
    i = pl.program_id(0)
    q = CHUNK
    proj0, proj1 = (z0_ref, xbc0_ref, dt0_ref), (z1_ref, xbc1_ref, dt1_ref)
    weights = (g_ref, wz_ref, wx_ref, wdt_ref)
    consts = (convw_ref, convb_ref, dtb_ref, alog_ref, dskip_ref, gnorm_ref, wout_ref)

    @pl.when(i % steps_per_seq == 0)
    def _():
        xpad_ref[0:8, :] = jnp.zeros((8, CONV_DIM), F32)
        st_ref[...] = jnp.zeros_like(st_ref)

    @pl.when(i == 0)
    def _():
        _project_chunk(x2_ref[0:q, :], *weights, proj0)

    h_ref[0:q, :] = _scan_chunk(proj0, x2_ref[0:q, :], *consts, xpad_ref, st_ref, (act0_ref, gate0_ref, y0_ref),
                                x2_ref[q:2 * q, :], *weights, proj1)
    h_ref[q:2 * q, :] = _scan_chunk(proj1, x2_ref[q:2 * q, :], *consts, xpad_ref, st_ref,
                                    (act1_ref, gate1_ref, y1_ref), xnext_ref[...], *weights, proj0)

    @pl.when(i % steps_per_seq == steps_per_seq - 1)
    def _():
        conv_out_ref[0] = xpad_ref[5:8, :]
        for j in range(SSM_HEADS // 2):
            ssm_out_ref[0, 2 * j:2 * j + 2] = st_ref[j].T.reshape(2, SSM_HEAD_DIM, D_STATE)


def _ssd_prompt(x, b, g, wz, wx, wdt, convw, convb, dtb, alog, dskip, gnorm, wout):
    m = x.shape[0]
    steps = m // (2 * CHUNK)
    steps_per_seq = steps // b
    n_chunks = m // CHUNK
    vm = lambda *shape: pltpu.VMEM(shape, F32)
    per_seq = lambda *blk: pl.BlockSpec((1,) + blk, lambda i: (i // steps_per_seq,) + (0,) * len(blk))
    return pl.pallas_call(
        functools.partial(_ssd_prompt_kernel, steps_per_seq=steps_per_seq),
        grid=(steps,),
        in_specs=[pl.BlockSpec((2 * CHUNK, D_MODEL), lambda i: (i, 0)),
                  pl.BlockSpec((CHUNK, D_MODEL), lambda i: (jnp.minimum(2 * i + 2, n_chunks - 1), 0)),
                  _full(g.shape), _full(wz.shape), _full(wx.shape), _full(wdt.shape),
                  _full(convw.shape), _full(convb.shape), _full(dtb.shape), _full(alog.shape),
                  _full(dskip.shape), _full(gnorm.shape), _full(wout.shape)],
        out_specs=[pl.BlockSpec((2 * CHUNK, D_MODEL), lambda i: (i, 0)),
                   per_seq(D_CONV - 1, CONV_DIM),
                   per_seq(SSM_HEADS, SSM_HEAD_DIM, D_STATE)],
        out_shape=[jax.ShapeDtypeStruct((m, D_MODEL), F32),
                   jax.ShapeDtypeStruct((b, D_CONV - 1, CONV_DIM), F32),
                   jax.ShapeDtypeStruct((b, SSM_HEADS, SSM_HEAD_DIM, D_STATE), F32)],
        scratch_shapes=[vm(CHUNK, D_INNER), vm(CHUNK, CONV_DIM), vm(CHUNK, LANES),
                        vm(CHUNK, D_INNER), vm(CHUNK, CONV_DIM), vm(CHUNK, LANES),
                        vm(CHUNK, CONV_DIM), vm(CHUNK, D_INNER), vm(CHUNK, D_INNER),
                        vm(CHUNK, CONV_DIM), vm(CHUNK, D_INNER), vm(CHUNK, D_INNER),
                        vm(CHUNK + 8, CONV_DIM),
                        vm(SSM_HEADS // 2, D_STATE, LANES)],
        compiler_params=_params("arbitrary"),
        name="ssd_prompt",
    )(x, x, g, wz, wx, wdt, convw, convb, dtb, alog, dskip, gnorm, wout)


def _ssd_step_pre_kernel(xbc_ref, dt_ref, sconv_ref, convw_ref, convb_ref, dtb_ref, alog_ref,
                         hexp_ref, gexp_ref,
                         conv_out_ref, xdt_ref, dec_ref, bm_ref, cm_ref, ydiag_ref, xs_ref):
    xbc = xbc_ref[...]
    conv = convb_ref[...] + xbc * convw_ref[3:4, :]
    for k in range(D_CONV - 1):
        conv = conv + sconv_ref[k] * convw_ref[k:k + 1, :]
    conv_out_ref[0] = sconv_ref[1]
    conv_out_ref[1] = sconv_ref[2]
    conv_out_ref[2] = xbc
    act = _silu(conv)
    xs = act[:, :D_INNER]
    bm = act[:, D_INNER:D_INNER + N_GROUPS * D_STATE]
    cm = act[:, D_INNER + N_GROUPS * D_STATE:]
    dt = _softplus(dt_ref[...] + dtb_ref[...])
    dec = jnp.exp(dt * -jnp.exp(alog_ref[...]))
    hexp = hexp_ref[...]
    dt_x = jnp.dot(dt, hexp, precision=HIGHEST, preferred_element_type=F32)
    dec_x = jnp.dot(dec, hexp, precision=HIGHEST, preferred_element_type=F32)
    prod = bm * cm
    cb = jnp.concatenate([jnp.sum(prod[:, g * D_STATE:(g + 1) * D_STATE], axis=1, keepdims=True)
                          for g in range(N_GROUPS)], axis=1)
    cb_x = jnp.dot(cb, gexp_ref[...], precision=HIGHEST, preferred_element_type=F32)
    xdt = xs * dt_x
    xdt_ref[...] = xdt
    dec_ref[...] = dec_x
    bm_ref[...] = bm
    cm_ref[...] = cm
    ydiag_ref[...] = cb_x * xdt
    xs_ref[...] = xs


def _ssd_step_pre(xbc, dt, sconv, convw, convb, dtb, alog, hexp, gexp):
    m = xbc.shape[0]
    gs = N_GROUPS * D_STATE
    return pl.pallas_call(
        _ssd_step_pre_kernel,
        out_shape=[jax.ShapeDtypeStruct((D_CONV - 1, m, CONV_DIM), F32),
                   jax.ShapeDtypeStruct((m, D_INNER), F32),
                   jax.ShapeDtypeStruct((m, D_INNER), F32),
                   jax.ShapeDtypeStruct((m, gs), F32),
                   jax.ShapeDtypeStruct((m, gs), F32),
                   jax.ShapeDtypeStruct((m, D_INNER), F32),
                   jax.ShapeDtypeStruct((m, D_INNER), F32)],
        compiler_params=pltpu.CompilerParams(vmem_limit_bytes=VMEM_LIMIT),
        name="ssd_step_pre",
    )(xbc, dt, sconv, convw, convb, dtb, alog, hexp, gexp)


def _ssd_step_state_kernel(xdt_ref, dec_ref, bm_ref, cm_ref, st_ref, yoff_ref, st_out_ref):
    gw = D_INNER // N_GROUPS
    hg = SSM_HEADS // N_GROUPS
    sub = lax.broadcasted_iota(jnp.int32, (8, gw), 0)
    sub_n = lax.broadcasted_iota(jnp.int32, (8, D_STATE), 0)
    ones_row = jnp.where(sub_n == 0, 1.0, 0.0).astype(F32)
    for g in range(N_GROUPS):
        xdt8 = jnp.where(sub == 0, xdt_ref[0, :, g * gw:(g + 1) * gw], 0.0)
        dec8 = jnp.where(sub == 0, dec_ref[0, :, g * gw:(g + 1) * gw], 0.0)
        bm8 = jnp.where(sub_n == 0, bm_ref[0, :, g * D_STATE:(g + 1) * D_STATE], 0.0)
        cm8 = jnp.broadcast_to(cm_ref[0, :, g * D_STATE:(g + 1) * D_STATE], (8, D_STATE)).astype(BF16)
        st = st_ref[0, g * hg:(g + 1) * hg].reshape(gw, D_STATE)
        y8 = lax.dot_general(cm8, st.astype(BF16), NT_DIMS, preferred_element_type=F32)
        yoff_ref[0, :, g * gw:(g + 1) * gw] = y8[0:1, :]
        dec_col = lax.dot_general(dec8, ones_row, TN_DIMS, precision=HIGHEST, preferred_element_type=F32)
        upd = lax.dot_general(xdt8, bm8, TN_DIMS, precision=HIGHEST, preferred_element_type=F32)
        st_out_ref[0, g * hg:(g + 1) * hg] = (st * dec_col + upd).reshape(hg, SSM_HEAD_DIM, D_STATE)


def _ssd_step_state(xdt, dec, bm, cm, st):
    m = xdt.shape[0]
    gs = N_GROUPS * D_STATE
    row = lambda n: pl.BlockSpec((1, 1, n), lambda i: (i, 0, 0))
    st_spec = pl.BlockSpec((1, SSM_HEADS, SSM_HEAD_DIM, D_STATE), lambda i: (i, 0, 0, 0))
    yoff, st_out = pl.pallas_call(
        _ssd_step_state_kernel,
        grid=(m,),
        in_specs=[row(D_INNER), row(D_INNER), row(gs), row(gs), st_spec],
        out_specs=[row(D_INNER), st_spec],
        out_shape=[jax.ShapeDtypeStruct((m, 1, D_INNER), F32),
                   jax.ShapeDtypeStruct(st.shape, F32)],
        compiler_params=_params("arbitrary"),
        name="ssd_step_state",
    )(xdt.reshape(m, 1, D_INNER), dec.reshape(m, 1, D_INNER), bm.reshape(m, 1, gs), cm.reshape(m, 1, gs), st)
    return yoff.reshape(m, D_INNER), st_out


def _ssd_step_out_kernel(ydiag_ref, yoff_ref, dec_ref, xs_ref, z_ref, x_ref, dskip_ref, gnorm_ref, wout_ref,
                         h_ref):
    y = ydiag_ref[...] + yoff_ref[...] * dec_ref[...] + dskip_ref[...] * xs_ref[...]
    h_ref[...] = _gate_norm_outproj(y, _silu(z_ref[...]), gnorm_ref[...], wout_ref[...], x_ref[...])


def _ssd_step_out(ydiag, yoff, dec, xs, z, x, dskip, gnorm, wout):
    return pl.pallas_call(
        _ssd_step_out_kernel,
        out_shape=jax.ShapeDtypeStruct(x.shape, F32),
        compiler_params=pltpu.CompilerParams(vmem_limit_bytes=VMEM_LIMIT),
        name="ssd_step_out",
    )(ydiag, yoff, dec, xs, z, x, dskip, gnorm, wout)


def _rope(t, cos, sin_up, sin_dn):
    cols = []
    for c in range(t.shape[1] // LANES):
        tc = t[:, c * LANES:(c + 1) * LANES]
        cols.append(tc * cos + pltpu.roll(tc, ROT_DIM // 2, 1) * sin_up
                    + pltpu.roll(tc, LANES - ROT_DIM // 2, 1) * sin_dn)
    return jnp.concatenate(cols, axis=1)


def _kvq_kernel(h_ref, gkv_ref, gb_ref, wkv_ref, wb_ref, cos_ref, sup_ref, sdn_ref, *out_refs, for_flash):
    xh = _rms(h_ref[...])
    kv = _dot((xh * gkv_ref[...]).astype(BF16), wkv_ref[...])
    qg = _dot((xh * gb_ref[...]).astype(BF16), wb_ref[...])
    cos, sup, sdn = cos_ref[...], sup_ref[...], sdn_ref[...]
    k = _rope(kv[:, :K_DIM], cos, sup, sdn)
    v = kv[:, K_DIM:]
    q = _rope(qg[:, :K_DIM], cos, sup, sdn)
    if for_flash:
        kt_ref, v_ref, kb_ref, vt_ref, qt_ref, gate_ref = out_refs
        kt_ref[0] = k.T
        kb_ref[...] = k.astype(BF16)
        vt = v.T.astype(BF16)
        for h in range(ATT_HEADS):
            vt_ref[0, 0, h, :V_DIM, :] = vt[h * V_DIM:(h + 1) * V_DIM]
            vt_ref[0, 0, h, V_DIM:, :] = jnp.ones((VT_ROWS - V_DIM, vt.shape[1]), BF16)
        qt_ref[0] = (q * Q_SCALE_LOG2).T.astype(BF16)
    else:
        k_ref, v_ref, q_ref, gate_ref = out_refs
        k_ref[...] = k
        q_ref[...] = q
    v_ref[...] = v
    gate_ref[...] = qg[:, K_DIM:]


def _kvq_in_specs(tm, wkv, wb, table_blocks):
    table = pl.BlockSpec((tm, LANES), lambda i: (i % table_blocks, 0))
    return [pl.BlockSpec((tm, D_MODEL), lambda i: (i, 0)), _full((1, D_MODEL)), _full((1, D_MODEL)),
            _full(wkv.shape), _full(wb.shape), table, table, table]


def _kvq(h, gkv, gb, wkv, wb, cos, sup, sdn, tm):
    m = h.shape[0]
    row = pl.BlockSpec((tm, D_MODEL), lambda i: (i, 0))
    out = jax.ShapeDtypeStruct((m, D_MODEL), F32)
    return pl.pallas_call(
        functools.partial(_kvq_kernel, for_flash=False),
        grid=(m // tm,),
        in_specs=_kvq_in_specs(tm, wkv, wb, cos.shape[0] // tm),
        out_specs=[row] * 4,
        out_shape=[out] * 4,
        compiler_params=_params("arbitrary"),
        name="kvq_proj",
    )(h, gkv, gb, wkv, wb, cos, sup, sdn)


def _kvq_flash(h, gkv, gb, wkv, wb, cos, sup, sdn, b, l):
    tm = FLASH_TK
    nb = l // tm
    row = pl.BlockSpec((tm, D_MODEL), lambda i: (i, 0))
    feat = pl.BlockSpec((1, D_MODEL, tm), lambda i: (i // nb, 0, i % nb))
    return pl.pallas_call(
        functools.partial(_kvq_kernel, for_flash=True),
        grid=(b * nb,),
        in_specs=_kvq_in_specs(tm, wkv, wb, nb),
        out_specs=[feat, row, row,
                   pl.BlockSpec((1, 1, ATT_HEADS, VT_ROWS, tm), lambda i: (i // nb, i % nb, 0, 0, 0)), feat, row],
        out_shape=[jax.ShapeDtypeStruct((b, K_DIM, l), F32),
                   jax.ShapeDtypeStruct((b * l, VAL_WIDTH), F32),
                   jax.ShapeDtypeStruct((b * l, K_DIM), BF16),
                   jax.ShapeDtypeStruct((b, nb, ATT_HEADS, VT_ROWS, tm), BF16),
                   jax.ShapeDtypeStruct((b, K_DIM, l), BF16),
                   jax.ShapeDtypeStruct((b * l, VAL_WIDTH), F32)],
        compiler_params=_params("arbitrary"),
        name="kvq_proj_flash",
    )(h, gkv, gb, wkv, wb, cos, sup, sdn)


def _rope_tables(pos):
    inv_freq = ROPE_THETA ** (-jnp.arange(0, ROT_DIM, 2, dtype=F32) / ROT_DIM)
    ang = pos[:, None] * inv_freq[None, :]
    cos, sin = jnp.cos(ang), jnp.sin(ang)
    half = ROT_DIM // 2
    rows = pos.shape[0]
    pad = jnp.zeros((rows, HEAD_DIM - ROT_DIM), F32)
    zero = jnp.zeros((rows, half), F32)
    cos64 = jnp.concatenate([cos, cos, pad + 1.0], axis=1)
    sup64 = jnp.concatenate([zero, sin, pad], axis=1)
    sdn64 = jnp.concatenate([-sin, zero, pad], axis=1)
    return tuple(jnp.tile(t, (1, LANES // HEAD_DIM)) for t in (cos64, sup64, sdn64))


def _lambda(lq1_ref, lk1_ref, lq2_ref, lk2_ref):
    s1 = jnp.sum(lq1_ref[...] * lk1_ref[...], axis=1, keepdims=True)
    s2 = jnp.sum(lq2_ref[...] * lk2_ref[...], axis=1, keepdims=True)
    return jnp.exp(s1) - jnp.exp(s2) + LAMBDA_INIT


def _sub_norm(o, subln):
    return _rms(o) * subln * (1.0 - LAMBDA_INIT)


def _flash_kernel(qt_ref, k_ref, vt_ref, lq1_ref, lk1_ref, lq2_ref, lk2_ref, subln_ref, o_ref,
                  acc_ref, sa_ref, sb_ref, *, tq, tk):
    qi = pl.program_id(2)
    n = 2 * tq
    qt = qt_ref[0]
    feat = lax.broadcasted_iota(jnp.int32, (LANES, tq), 0)
    zero = jnp.zeros_like(qt)
    qqt = jnp.concatenate([jnp.where(feat < HEAD_DIM, qt, zero), jnp.where(feat < HEAD_DIM, zero, qt)], axis=1)
    acc_ref[...] = jnp.zeros_like(acc_ref)

    def logits(kb, lanes=slice(None)):
        start = pl.multiple_of(kb * tk, tk)
        return _dot(k_ref[0, pl.ds(start, tk), :], qqt[:, lanes])

    def softmax_pv(st, m, kb, lanes=slice(None)):
        m_new = jnp.maximum(m, jnp.max(st, axis=0, keepdims=True))
        alpha = jnp.exp2(m - m_new)
        p = jnp.exp2(st - m_new).astype(BF16)
        acc_ref[:, lanes] = acc_ref[:, lanes] * alpha + _dot(vt_ref[0, kb, 0], p)
        return m_new

    sa_ref[...] = logits(0)

    def pair(i, m):
        kb = 2 * i
        sb_ref[...] = logits(kb + 1)
        m = softmax_pv(sa_ref[...], m, kb)
        sa_ref[...] = logits(kb + 2)
        return softmax_pv(sb_ref[...], m, kb + 1)

    m = lax.fori_loop(0, qi, pair, jnp.full((1, n), NEG_INF, F32))
    kb = 2 * qi
    late = (slice(tk, tq), slice(tq + tk, n))
    tri = lax.broadcasted_iota(jnp.int32, (tk, tk), 0) <= lax.broadcasted_iota(jnp.int32, (tk, tk), 1)
    late_logits = [jnp.where(tri, logits(kb + 1, lanes), NEG_INF) for lanes in late]
    key_i = lax.broadcasted_iota(jnp.int32, (tk, n), 0)
    qry_i = lax.broadcasted_iota(jnp.int32, (tk, n), 1) % tq
    m = softmax_pv(jnp.where(key_i <= qry_i, sa_ref[...], NEG_INF), m, kb)
    for lanes, st in zip(late, late_logits):
        softmax_pv(st, m[:, lanes], kb + 1, lanes)

    on = acc_ref[:V_DIM, :] * (1.0 / acc_ref[V_DIM:V_DIM + 1, :])
    ot = on[:, :tq] - _lambda(lq1_ref, lk1_ref, lq2_ref, lk2_ref) * on[:, tq:]
    ot = ot * lax.rsqrt(jnp.mean(ot * ot, axis=0, keepdims=True) + EPS) * subln_ref[...] * (1.0 - LAMBDA_INIT)
    o_ref[0] = ot.T


def _flash(qt, kb, vt, lq1, lk1, lq2, lk2, subln_col, tq):
    b, _, l = qt.shape
    tk = FLASH_TK
    assert tq == 2 * tk
    small = _full((1, HEAD_DIM))
    return pl.pallas_call(
        functools.partial(_flash_kernel, tq=tq, tk=tk),
        grid=(b, ATT_HEADS, l // tq),
        in_specs=[pl.BlockSpec((1, LANES, tq), lambda i, h, t: (i, h, t)),
                  pl.BlockSpec((1, l, LANES), lambda i, h, t: (i, 0, h)),
                  pl.BlockSpec((1, l // tk, 1, VT_ROWS, tk), lambda i, h, t: (i, 0, h, 0, 0)),
                  small, small, small, small, _full((V_DIM, 1))],
        out_specs=pl.BlockSpec((1, tq, LANES), lambda i, h, t: (i, t, h)),
        out_shape=jax.ShapeDtypeStruct((b, l, VAL_WIDTH), F32),
        scratch_shapes=[pltpu.VMEM((VT_ROWS, 2 * tq), F32),
                        pltpu.VMEM((tk, 2 * tq), F32),
                        pltpu.VMEM((tk, 2 * tq), F32)],
        compiler_params=_params("arbitrary", "arbitrary", "arbitrary"),
        name="flash_diff",
    )(qt, kb, vt, lq1, lk1, lq2, lk2, subln_col)


PAGES_PER_STEP = 16
N_MAPS = 2 * ATT_HEADS


def _paged_kernel(pt_ref, q_ref, kn_ref, vn_ref, lq1_ref, lk1_ref, lq2_ref, lk2_ref, subln_ref, *rest):
    kt_refs = rest[:PAGES_PER_STEP]
    v_refs = rest[PAGES_PER_STEP:2 * PAGES_PER_STEP]
    o_ref = rest[2 * PAGES_PER_STEP]
    qrow_ref, m_ref, l_ref, acc_ref = rest[2 * PAGES_PER_STEP + 1:]
    step = pl.program_id(1)
    head_of_map = lax.broadcasted_iota(jnp.int32, (N_MAPS, PAGE_SIZE), 0) % ATT_HEADS

    @pl.when(step == 0)
    def _():
        map_i = lax.broadcasted_iota(jnp.int32, (N_MAPS, K_DIM), 0)
        seg = 2 * (map_i % ATT_HEADS) + map_i // ATT_HEADS
        lane_seg = lax.broadcasted_iota(jnp.int32, (N_MAPS, K_DIM), 1) // HEAD_DIM
        qs = q_ref[0] * (HEAD_DIM ** -0.5)
        qrow = jnp.where(lane_seg == seg, jnp.broadcast_to(qs, (N_MAPS, K_DIM)), 0.0)
        qrow_ref[...] = qrow.astype(BF16)
        m_ref[...] = jnp.sum(qrow * kn_ref[0], axis=1, keepdims=True)
        l_ref[...] = jnp.ones_like(l_ref)
        acc_ref[...] = jnp.concatenate([vn_ref[0], vn_ref[0]], axis=0)

    qrow = qrow_ref[...]
    s = jnp.concatenate([_dot(qrow, kt_refs[i][0].astype(BF16)) for i in range(PAGES_PER_STEP)], axis=1)
    m_old = m_ref[...]
    m_new = jnp.maximum(m_old, jnp.max(s, axis=1, keepdims=True))
    alpha = jnp.exp(m_old - m_new)
    p = jnp.exp(s - m_new)
    l_ref[...] = alpha * l_ref[...] + jnp.sum(p, axis=1, keepdims=True)
    m_ref[...] = m_new
    pv = jnp.zeros((N_MAPS, V_DIM), F32)
    for i in range(PAGES_PER_STEP):
        p_i = p[:, i * PAGE_SIZE:(i + 1) * PAGE_SIZE]
        p_heads = jnp.concatenate([jnp.where(head_of_map == h, p_i, 0.0) for h in range(ATT_HEADS)], axis=1)
        v_heads = jnp.concatenate([v_refs[i][0, pl.ds(h, PAGE_SIZE, stride=ATT_HEADS), :]
                                   for h in range(ATT_HEADS)], axis=0)
        pv = pv + _dot(p_heads.astype(BF16), v_heads.astype(BF16))
    acc_ref[...] = alpha * acc_ref[...] + pv

    @pl.when(step == pl.num_programs(1) - 1)
    def _():
        on = acc_ref[...] / l_ref[...]
        o = on[:ATT_HEADS] - _lambda(lq1_ref, lk1_ref, lq2_ref, lk2_ref) * on[ATT_HEADS:]
        o_ref[0] = _sub_norm(o, subln_ref[...])


def _paged(page_table, q, kn, vn, cache_kt, cache_v, lq1, lk1, lq2, lk2, subln):
    m = q.shape[0]
    n_steps = N_PAGES // PAGES_PER_STEP
    row = pl.BlockSpec((1, 1, D_MODEL), lambda i, s, pt: (i, 0, 0))
    heads = pl.BlockSpec((1, ATT_HEADS, V_DIM), lambda i, s, pt: (i, 0, 0))
    small = pl.BlockSpec((1, HEAD_DIM), lambda i, s, pt: (0, 0))

    def page(r):
        return pl.BlockSpec((1, K_DIM, PAGE_SIZE),
                            lambda i, s, pt: (pt[i * N_PAGES + s * PAGES_PER_STEP + r], 0, 0))

    pages = [page(r) for r in range(PAGES_PER_STEP)]
    grid_spec = pltpu.PrefetchScalarGridSpec(
        num_scalar_prefetch=1,
        grid=(m, n_steps),
        in_specs=[row, row, heads, small, small, small, small,
                  pl.BlockSpec((1, V_DIM), lambda i, s, pt: (0, 0))] + pages + pages,
        out_specs=heads,
        scratch_shapes=[pltpu.VMEM((N_MAPS, K_DIM), BF16),
                        pltpu.VMEM((N_MAPS, 1), F32),
                        pltpu.VMEM((N_MAPS, 1), F32),
                        pltpu.VMEM((N_MAPS, V_DIM), F32)],
    )
    out = pl.pallas_call(
        _paged_kernel,
        grid_spec=grid_spec,
        out_shape=jax.ShapeDtypeStruct((m, ATT_HEADS, V_DIM), F32),
        compiler_params=_params("arbitrary", "arbitrary"),
        name="paged_diff",
    )(page_table.reshape(-1), q.reshape(m, 1, K_DIM), kn.reshape(m, 1, K_DIM), vn.reshape(m, ATT_HEADS, V_DIM),
      lq1, lk1, lq2, lk2, subln, *([cache_kt] * PAGES_PER_STEP), *([cache_v] * PAGES_PER_STEP))
    return out.reshape(m, VAL_WIDTH)


def _attn_out_kernel(o_ref, gate_ref, h_ref, wout_ref, gf_ref, y_ref):
    og = (o_ref[...] * _silu(gate_ref[...])).astype(BF16)
    y_ref[...] = _rms(h_ref[...] + _dot(og, wout_ref[...])) * gf_ref[...]


def _attn_out(o, gate, h, wout, gf, tm):
    m = o.shape[0]
    row = pl.BlockSpec((tm, D_MODEL), lambda i: (i, 0))
    return pl.pallas_call(
        _attn_out_kernel,
        grid=(m // tm,),
        in_specs=[row, row, row, _full(wout.shape), _full((1, D_MODEL))],
        out_specs=row,
        out_shape=jax.ShapeDtypeStruct((m, D_MODEL), F32),
        compiler_params=_params("arbitrary"),
        name="attn_out",
    )(o, gate, h, wout, gf)


def _pad_lanes(v):
    return jnp.pad(v.reshape(1, -1), ((0, 0), (0, LANES - v.shape[-1])))


def kernel(x_prompt, x_sample, cache_k, cache_v, page_table, state_conv, state_ssm, norm_a, w_in_a, conv_w, conv_b, dt_bias, a_log, d_skip, gnorm_a, w_out_a, norm_kv, w_kv, norm_b, w_in_b, lambda_q1, lambda_k1, lambda_q2, lambda_k2, subln_b, w_out_b, norm_f):
    bp, lp, _ = x_prompt.shape
    bs = x_sample.shape[0]
    mp = bp * lp

    w_a = w_in_a[0].astype(BF16)
    wz, wx = w_a[:, :D_INNER], w_a[:, D_INNER:D_INNER + CONV_DIM]
    wdt = jnp.pad(w_a[:, D_INNER + CONV_DIM:], ((0, 0), (0, LANES - SSM_HEADS)))
    wout_a = w_out_a[0].astype(BF16)
    wkv = w_kv.astype(BF16)
    wb = w_in_b[0].astype(BF16)
    wout_b = w_out_b[0].astype(BF16)
    g_a = norm_a[0].reshape(1, D_MODEL)
    convw, convb = conv_w[0], conv_b[0].reshape(1, CONV_DIM)
    dtb, alog = _pad_lanes(dt_bias[0]), _pad_lanes(a_log[0])
    dskip = jnp.repeat(d_skip[0], SSM_HEAD_DIM).reshape(1, D_INNER)
    gnorm = gnorm_a[0].reshape(1, D_INNER)
    g_kv, g_b, g_f = norm_kv.reshape(1, D_MODEL), norm_b[0].reshape(1, D_MODEL), norm_f.reshape(1, D_MODEL)
    lq1, lk1, lq2, lk2 = (t[0].reshape(1, HEAD_DIM) for t in (lambda_q1, lambda_k1, lambda_q2, lambda_k2))
    subln = subln_b[0].reshape(1, V_DIM)
    head_of_lane = jnp.arange(D_INNER) // SSM_HEAD_DIM
    hexp = (jnp.arange(LANES)[:, None] == head_of_lane[None, :]).astype(F32)
    gexp = (jnp.arange(N_GROUPS)[:, None] == (head_of_lane // (SSM_HEADS // N_GROUPS))[None, :]).astype(F32)

    xp = x_prompt.reshape(mp, D_MODEL)
    h1, conv_p, ssm_p = _ssd_prompt(xp, bp, g_a, wz, wx, wdt, convw, convb, dtb, alog, dskip, gnorm, wout_a)
    tables_p = _rope_tables(jnp.arange(lp, dtype=F32))
    kt_p, v_p, kb_p, vt_p, qt_p, gate_p = _kvq_flash(h1, g_kv, g_b, wkv, wb, *tables_p, b=bp, l=lp)
    o_p = _flash(qt_p, kb_p.reshape(bp, lp, K_DIM), vt_p, lq1, lk1, lq2, lk2, subln.reshape(V_DIM, 1), tq=512)
    y_p = _attn_out(o_p.reshape(mp, VAL_WIDTH), gate_p, h1, wout_b, g_f, tm=256)
    k_p = jnp.transpose(kt_p.reshape(bp, ATT_HEADS, 2, HEAD_DIM, lp), (0, 4, 1, 2, 3))

    n_pool = cache_k.shape[0]
    cache_kt = jnp.transpose(cache_k, (0, 2, 3, 4, 1)).reshape(n_pool, K_DIM, PAGE_SIZE)
    cache_vr = cache_v.reshape(n_pool, PAGE_SIZE * ATT_HEADS, V_DIM)
    sconv = jnp.transpose(state_conv[0], (1, 0, 2))
    xs_in = x_sample.reshape(bs, D_MODEL)
    z_s, xbc_s, dt_s = _inproj_a(xs_in, g_a, wz, wx, wdt, tm=bs)
    conv_s, xdt, dec, bm, cm, ydiag, xs_act = _ssd_step_pre(xbc_s, dt_s, sconv, convw, convb, dtb, alog, hexp, gexp)
    yoff, ssm_s = _ssd_step_state(xdt, dec, bm, cm, state_ssm[0])
    h1_s = _ssd_step_out(ydiag, yoff, dec, xs_act, z_s, xs_in, dskip, gnorm, wout_a)
    tables_s = _rope_tables(jnp.full((bs,), float(PAST_LEN), F32))
    k_s, v_s, q_s, gate_s = _kvq(h1_s, g_kv, g_b, wkv, wb, *tables_s, tm=bs)
    o_s = _paged(page_table, q_s, k_s, v_s, cache_kt, cache_vr, lq1, lk1, lq2, lk2, subln)
    y_s = _attn_out(o_s, gate_s, h1_s, wout_b, g_f, tm=bs)

    return (y_p.reshape(bp, lp, D_MODEL), y_s.reshape(bs, 1, D_MODEL),
            k_p, v_p.reshape(bp, lp, ATT_HEADS, V_DIM),
            conv_p[None], ssm_p[None],
            k_s.reshape(bs, 1, ATT_HEADS, 2, HEAD_DIM), v_s.reshape(bs, 1, ATT_HEADS, V_DIM),
            jnp.transpose(conv_s, (1, 0, 2))[None], ssm_s[None])
```

```python
import functools
import math

import jax
import jax.numpy as jnp
from jax import lax
from jax.experimental import pallas as pl
from jax.experimental.pallas import tpu as pltpu

F32 = jnp.float32
BF16 = jnp.bfloat16
HIGHEST = lax.Precision.HIGHEST

D_MODEL = 1024
PAST_LEN = 8192
PAGE_SIZE = 128
N_PAGES = PAST_LEN // PAGE_SIZE
D_INNER = 2048
SSM_HEAD_DIM = 64
SSM_HEADS = 32
N_GROUPS = 4
D_STATE = 128
D_CONV = 4
CONV_DIM = D_INNER + 2 * N_GROUPS * D_STATE
CHUNK = 128
ATT_HEADS = 8
HEAD_DIM = 64
V_DIM = 128
K_DIM = 1024
VAL_WIDTH = 1024
ROT_DIM = 16
ROPE_THETA = 500000.0
EPS = 1e-6
NEG_INF = -1e30
LAMBDA_INIT = 0.8 - 0.6 * math.exp(-0.3 * 1)

FLASH_TK = 256
VT_ROWS = V_DIM + 16
Q_SCALE_LOG2 = HEAD_DIM ** -0.5 * math.log2(math.e)

LANES = 128
VMEM_LIMIT = 56 * 1024 * 1024

NT_DIMS = (((1,), (1,)), ((), ()))
TN_DIMS = (((0,), (0,)), ((), ()))


def _params(*sem):
    return pltpu.CompilerParams(dimension_semantics=sem, vmem_limit_bytes=VMEM_LIMIT)


def _silu(x):
    return x * (1.0 / (1.0 + jnp.exp(-x)))


def _softplus(x):
    return jnp.maximum(x, 0.0) + jnp.log1p(jnp.exp(-jnp.abs(x)))


def _rms(x):
    return x * lax.rsqrt(jnp.mean(x * x, axis=-1, keepdims=True) + EPS)


def _dot(a, b):
    return jnp.dot(a, b, preferred_element_type=F32)


def _full(shape):
    return pl.BlockSpec(shape, lambda *_: (0,) * len(shape))


def _inproj_a_kernel(x_ref, g_ref, wz_ref, wx_ref, wdt_ref, z_ref, xbc_ref, dt_ref):
    xn = (_rms(x_ref[...]) * g_ref[...]).astype(BF16)
    z_ref[...] = _dot(xn, wz_ref[...])
    xbc_ref[...] = _dot(xn, wx_ref[...])
    dt_ref[...] = _dot(xn, wdt_ref[...])


def _inproj_a(x, g, wz, wx, wdt, tm):
    m = x.shape[0]
    row = lambda n: pl.BlockSpec((tm, n), lambda i: (i, 0))
    return pl.pallas_call(
        _inproj_a_kernel,
        grid=(m // tm,),
        in_specs=[row(D_MODEL), _full((1, D_MODEL)), _full(wz.shape), _full(wx.shape), _full(wdt.shape)],
        out_specs=[row(D_INNER), row(CONV_DIM), row(LANES)],
        out_shape=[jax.ShapeDtypeStruct((m, D_INNER), F32),
                   jax.ShapeDtypeStruct((m, CONV_DIM), F32),
                   jax.ShapeDtypeStruct((m, LANES), F32)],
        compiler_params=_params("arbitrary"),
        name="inproj_a",
    )(x, g, wz, wx, wdt)


def _gate_norm_outproj(y, gate, gnorm, wout, xres):
    yg = y * gate
    gw = D_INNER // N_GROUPS
    parts = [_rms(yg[:, g * gw:(g + 1) * gw]) for g in range(N_GROUPS)]
    yn = (jnp.concatenate(parts, axis=1) * gnorm).astype(BF16)
    return xres + _dot(yn, wout)


def _project_chunk(x, g_ref, wz_ref, wx_ref, wdt_ref, proj):
    z_ref, xbc_ref, dt_ref = proj
    xn = (_rms(x) * g_ref[...]).astype(BF16)
    z_ref[...] = _dot(xn, wz_ref[...])
    xbc_ref[...] = _dot(xn, wx_ref[...])
    dt_ref[...] = _dot(xn, wdt_ref[...])


PROJ_COLS = 512


def _scan_chunk(proj, xres, convw_ref, convb_ref, dtb_ref, alog_ref, dskip_ref, gnorm_ref, wout_ref,
                xpad_ref, st_ref, work, x_ahead, g_ref, wz_ref, wx_ref, wdt_ref, proj_ahead):
    z_ref, xbc_ref, dt_ref = proj
    za_ref, xbca_ref, dta_ref = proj_ahead
    act_ref, gate_ref, y_ref = work
    q = CHUNK
    xn = (_rms(x_ahead) * g_ref[...]).astype(BF16)
    sub8 = lax.broadcasted_iota(jnp.int32, (q // 8, 8, PROJ_COLS), 1)

    for blk in range(CONV_DIM // PROJ_COLS):
        cols = slice(blk * PROJ_COLS, (blk + 1) * PROJ_COLS)
        xbca_ref[:, cols] = _dot(xn, wx_ref[:, cols])
        xbc = xbc_ref[:, cols]
        xpad_ref[8:8 + q, cols] = xbc
        conv = (convb_ref[:, cols] + xbc * convw_ref[3:4, cols]).reshape(q // 8, 8, PROJ_COLS)
        groups = xpad_ref[:, cols].reshape(q // 8 + 1, 8, PROJ_COLS)
        for k in range(1, D_CONV):
            rot = pltpu.roll(groups, k, 1)
            back_k = jnp.where(sub8 < k, rot[:q // 8], rot[1:])
            conv = conv + back_k * convw_ref[D_CONV - 1 - k:D_CONV - k, cols]
        act_ref[:, cols] = _silu(conv).reshape(q, PROJ_COLS)
        xpad_ref[0:8, cols] = xpad_ref[q:q + 8, cols]
    for blk in range(D_INNER // PROJ_COLS):
        cols = slice(blk * PROJ_COLS, (blk + 1) * PROJ_COLS)
        za_ref[:, cols] = _dot(xn, wz_ref[:, cols])
        gate_ref[:, cols] = _silu(z_ref[:, cols])
    dta_ref[...] = _dot(xn, wdt_ref[...])

    dt = _softplus(dt_ref[...] + dtb_ref[...])
    a = -jnp.exp(alog_ref[...])
    row_i = lax.broadcasted_iota(jnp.int32, (q, q), 0)
    col_i = lax.broadcasted_iota(jnp.int32, (q, q), 1)
    causal = row_i >= col_i
    a_cs = jnp.dot(causal.astype(F32), dt * a, precision=HIGHEST, preferred_element_type=F32)
    a_cs_t = a_cs.T
    dt_t = dt.T
    w_t = dt_t * jnp.exp(a_cs_t[:, q - 1:q] - a_cs_t)
    e_cs = jnp.exp(a_cs)
    first_half = col_i < SSM_HEAD_DIM

    for g in range(N_GROUPS):
        bm = act_ref[:, D_INNER + g * D_STATE:D_INNER + (g + 1) * D_STATE]
        cm = act_ref[:, D_INNER + (N_GROUPS + g) * D_STATE:D_INNER + (N_GROUPS + g + 1) * D_STATE]
        bm_b = bm.astype(BF16)
        cm_b = cm.astype(BF16)
        cb = lax.dot_general(cm_b, bm_b, NT_DIMS, preferred_element_type=F32)
        bm_t = bm.T
        for jj in range(SSM_HEADS // N_GROUPS // 2):
            j = g * (SSM_HEADS // N_GROUPS // 2) + jj
            xs = act_ref[:, j * LANES:(j + 1) * LANES]
            xs_b = xs.astype(BF16)
            zero = jnp.zeros_like(xs_b)
            xbd = jnp.concatenate([jnp.where(first_half, xs_b, zero),
                                   jnp.where(first_half, zero, xs_b)], axis=0)
            m_parts, w_parts = [], []
            for h in (2 * j, 2 * j + 1):
                seg = a_cs[:, h:h + 1] - a_cs_t[h:h + 1, :]
                decay = jnp.exp(jnp.where(causal, seg, -jnp.inf))
                m_parts.append((cb * decay * dt_t[h:h + 1, :]).astype(BF16))
                w_parts.append((bm_t * w_t[h:h + 1, :]).astype(BF16))
            y_diag = _dot(jnp.concatenate(m_parts, axis=1), xbd)
            upd_t = _dot(jnp.concatenate(w_parts, axis=1), xbd)
            st = st_ref[j]
            e_sel = jnp.where(first_half, e_cs[:, 2 * j:2 * j + 1], e_cs[:, 2 * j + 1:2 * j + 2])
            y_off = _dot(cm_b, st.astype(BF16)) * e_sel
            st_ref[j] = st * e_sel[q - 1:q, :] + upd_t
            y_ref[:, j * LANES:(j + 1) * LANES] = y_diag + y_off + dskip_ref[:, j * LANES:(j + 1) * LANES] * xs

    return _gate_norm_outproj(y_ref[...], gate_ref[...], gnorm_ref[...], wout_ref[...], xres)


def _ssd_prompt_kernel(x2_ref, xnext_ref, g_ref, wz_ref, wx_ref, wdt_ref, convw_ref, convb_ref, dtb_ref, alog_ref,
                       dskip_ref, gnorm_ref, wout_ref,
                       h_ref, conv_out_ref, ssm_out_ref,
                       z0_ref, xbc0_ref, dt0_ref, z1_ref, xbc1_ref, dt1_ref,
                       act0_ref, gate0_ref, y0_ref, act1_ref, gate1_ref, y1_ref, xpad_ref, st_ref,
                       *, steps_per_seq):
    i = pl.program_id(0)
    q = CHUNK
    proj0, proj1 = (z0_ref, xbc0_ref, dt0_ref), (z1_ref, xbc1_ref, dt1_ref)
    weights = (g_ref, wz_ref, wx_ref, wdt_ref)
    consts = (convw_ref, convb_ref, dtb_ref, alog_ref, dskip_ref, gnorm_ref, wout_ref)

    @pl.when(i % steps_per_seq == 0)
    def _():
        xpad_ref[0:8, :] = jnp.zeros((8, CONV_DIM), F32)
        st_ref[...] = jnp.zeros_like(st_ref)

    @pl.when(i == 0)
    def _():
        _project_chunk(x2_ref[0:q, :], *weights, proj0)

    h_ref[0:q, :] = _scan_chunk(proj0, x2_ref[0:q, :], *consts, xpad_ref, st_ref, (act0_ref, gate0_ref, y0_ref),
                                x2_ref[q:2 * q, :], *weights, proj1)
    h_ref[q:2 * q, :] = _scan_chunk(proj1, x2_ref[q:2 * q, :], *consts, xpad_ref, st_ref,
                                    (act1_ref, gate1_ref, y1_ref), xnext_ref[...], *weights, proj0)

    @pl.when(i % steps_per_seq == steps_per_seq - 1)
    def _():
        conv_out_ref[0] = xpad_ref[5:8, :]
        for j in range(SSM_HEADS // 2):
            ssm_out_ref[0, 2 * j:2 * j + 2] = st_ref[j].T.reshape(2, SSM_HEAD_DIM, D_STATE)


def _ssd_prompt(x, b, g, wz, wx, wdt, convw, convb, dtb, alog, dskip, gnorm, wout):
    m = x.shape[0]
    steps = m // (2 * CHUNK)
    steps_per_seq = steps // b
    n_chunks = m // CHUNK
    vm = lambda *shape: pltpu.VMEM(shape, F32)
    per_seq = lambda *blk: pl.BlockSpec((1,) + blk, lambda i: (i // steps_per_seq,) + (0,) * len(blk))
    return pl.pallas_call(
        functools.partial(_ssd_prompt_kernel, steps_per_seq=steps_per_seq),
        grid=(steps,),
        in_specs=[pl.BlockSpec((2 * CHUNK, D_MODEL), lambda i: (i, 0)),
                  pl.BlockSpec((CHUNK, D_MODEL), lambda i: (jnp.minimum(2 * i + 2, n_chunks - 1), 0)),
                  _full(g.shape), _full(wz.shape), _full(wx.shape), _full(wdt.shape),
                  _full(convw.shape), _full(convb.shape), _full(dtb.shape), _full(alog.shape),
                  _full(dskip.shape), _full(gnorm.shape), _full(wout.shape)],
        out_specs=[pl.BlockSpec((2 * CHUNK, D_MODEL), lambda i: (i, 0)),
                   per_seq(D_CONV - 1, CONV_DIM),
                   per_seq(SSM_HEADS, SSM_HEAD_DIM, D_STATE)],
        out_shape=[jax.ShapeDtypeStruct((m, D_MODEL), F32),
                   jax.ShapeDtypeStruct((b, D_CONV - 1, CONV_DIM), F32),
                   jax.ShapeDtypeStruct((b, SSM_HEADS, SSM_HEAD_DIM, D_STATE), F32)],
        scratch_shapes=[vm(CHUNK, D_INNER), vm(CHUNK, CONV_DIM), vm(CHUNK, LANES),
                        vm(CHUNK, D_INNER), vm(CHUNK, CONV_DIM), vm(CHUNK, LANES),
                        vm(CHUNK, CONV_DIM), vm(CHUNK, D_INNER), vm(CHUNK, D_INNER),
                        vm(CHUNK, CONV_DIM), vm(CHUNK, D_INNER), vm(CHUNK, D_INNER),
                        vm(CHUNK + 8, CONV_DIM),
                        vm(SSM_HEADS // 2, D_STATE, LANES)],
        compiler_params=_params("arbitrary"),
        name="ssd_prompt",
    )(x, x, g, wz, wx, wdt, convw, convb, dtb, alog, dskip, gnorm, wout)


def _ssd_step_pre_kernel(xbc_ref, dt_ref, sconv_ref, convw_ref, convb_ref, dtb_ref, alog_ref,
                         hexp_ref, gexp_ref,
                         conv_out_ref, xdt_ref, dec_ref, bm_ref, cm_ref, ydiag_ref, xs_ref):
    xbc = xbc_ref[...]
    conv = convb_ref[...] + xbc * convw_ref[3:4, :]
    for k in range(D_CONV - 1):
        conv = conv + sconv_ref[k] * convw_ref[k:k + 1, :]
    conv_out_ref[0] = sconv_ref[1]
    conv_out_ref[1] = sconv_ref[2]
    conv_out_ref[2] = xbc
    act = _silu(conv)
    xs = act[:, :D_INNER]
    bm = act[:, D_INNER:D_INNER + N_GROUPS * D_STATE]
    cm = act[:, D_INNER + N_GROUPS * D_STATE:]
    dt = _softplus(dt_ref[...] + dtb_ref[...])
    dec = jnp.exp(dt * -jnp.exp(alog_ref[...]))
    hexp = hexp_ref[...]
    dt_x = jnp.dot(dt, hexp, precision=HIGHEST, preferred_element_type=F32)
    dec_x = jnp.dot(dec, hexp, precision=HIGHEST, preferred_element_type=F32)
    prod = bm * cm
    cb = jnp.concatenate([jnp.sum(prod[:, g * D_STATE:(g + 1) * D_STATE], axis=1, keepdims=True)
                          for g in range(N_GROUPS)], axis=1)
    cb_x = jnp.dot(cb, gexp_ref[...], precision=HIGHEST, preferred_element_type=F32)
    xdt = xs * dt_x
    xdt_ref[...] = xdt
    dec_ref[...] = dec_x
    bm_ref[...] = bm
    cm_ref[...] = cm
    ydiag_ref[...] = cb_x * xdt
    xs_ref[...] = xs


def _ssd_step_pre(xbc, dt, sconv, convw, convb, dtb, alog, hexp, gexp):
    m = xbc.shape[0]
    gs = N_GROUPS * D_STATE
    return pl.pallas_call(
        _ssd_step_pre_kernel,
        out_shape=[jax.ShapeDtypeStruct((D_CONV - 1, m, CONV_DIM), F32),
                   jax.ShapeDtypeStruct((m, D_INNER), F32),
                   jax.ShapeDtypeStruct((m, D_INNER), F32),
                   jax.ShapeDtypeStruct((m, gs), F32),
                   jax.ShapeDtypeStruct((m, gs), F32),
                   jax.ShapeDtypeStruct((m, D_INNER), F32),
                   jax.ShapeDtypeStruct((m, D_INNER), F32)],
        compiler_params=pltpu.CompilerParams(vmem_limit_bytes=VMEM_LIMIT),
        name="ssd_step_pre",
    )(xbc, dt, sconv, convw, convb, dtb, alog, hexp, gexp)


def _ssd_step_state_kernel(xdt_ref, dec_ref, bm_ref, cm_ref, st_ref, yoff_ref, st_out_ref):
    gw = D_INNER // N_GROUPS
    hg = SSM_HEADS // N_GROUPS
    sub = lax.broadcasted_iota(jnp.int32, (8, gw), 0)
    sub_n = lax.broadcasted_iota(jnp.int32, (8, D_STATE), 0)
    ones_row = jnp.where(sub_n == 0, 1.0, 0.0).astype(F32)
    for g in range(N_GROUPS):
        xdt8 = jnp.where(sub == 0, xdt_ref[0, :, g * gw:(g + 1) * gw], 0.0)
        dec8 = jnp.where(sub == 0, dec_ref[0, :, g * gw:(g + 1) * gw], 0.0)
        bm8 = jnp.where(sub_n == 0, bm_ref[0, :, g * D_STATE:(g + 1) * D_STATE], 0.0)
        cm8 = jnp.broadcast_to(cm_ref[0, :, g * D_STATE:(g + 1) * D_STATE], (8, D_STATE)).astype(BF16)
        st = st_ref[0, g * hg:(g + 1) * hg].reshape(gw, D_STATE)
        y8 = lax.dot_general(cm8, st.astype(BF16), NT_DIMS, preferred_element_type=F32)
        yoff_ref[0, :, g * gw:(g + 1) * gw] = y8[0:1, :]
        dec_col = lax.dot_general(dec8, ones_row, TN_DIMS, precision=HIGHEST, preferred_element_type=F32)
        upd = lax.dot_general(xdt8, bm8, TN_DIMS, precision=HIGHEST, preferred_element_type=F32)
        st_out_ref[0, g * hg:(g + 1) * hg] = (st * dec_col + upd).reshape(hg, SSM_HEAD_DIM, D_STATE)


def _ssd_step_state(xdt, dec, bm, cm, st):
    m = xdt.shape[0]
    gs = N_GROUPS * D_STATE
    row = lambda n: pl.BlockSpec((1, 1, n), lambda i: (i, 0, 0))
    st_spec = pl.BlockSpec((1, SSM_HEADS, SSM_HEAD_DIM, D_STATE), lambda i: (i, 0, 0, 0))
    yoff, st_out = pl.pallas_call(
        _ssd_step_state_kernel,
        grid=(m,),
        in_specs=[row(D_INNER), row(D_INNER), row(gs), row(gs), st_spec],
        out_specs=[row(D_INNER), st_spec],
        out_shape=[jax.ShapeDtypeStruct((m, 1, D_INNER), F32),
                   jax.ShapeDtypeStruct(st.shape, F32)],
        compiler_params=_params("arbitrary"),
        name="ssd_step_state",
    )(xdt.reshape(m, 1, D_INNER), dec.reshape(m, 1, D_INNER), bm.reshape(m, 1, gs), cm.reshape(m, 1, gs), st)
    return yoff.reshape(m, D_INNER), st_out


def _ssd_step_out_kernel(ydiag_ref, yoff_ref, dec_ref, xs_ref, z_ref, x_ref, dskip_ref, gnorm_ref, wout_ref,
                         h_ref):
    y = ydiag_ref[...] + yoff_ref[...] * dec_ref[...] + dskip_ref[...] * xs_ref[...]
    h_ref[...] = _gate_norm_outproj(y, _silu(z_ref[...]), gnorm_ref[...], wout_ref[...], x_ref[...])


def _ssd_step_out(ydiag, yoff, dec, xs, z, x, dskip, gnorm, wout):
    return pl.pallas_call(
        _ssd_step_out_kernel,
        out_shape=jax.ShapeDtypeStruct(x.shape, F32),
        compiler_params=pltpu.CompilerParams(vmem_limit_bytes=VMEM_LIMIT),
        name="ssd_step_out",
    )(ydiag, yoff, dec, xs, z, x, dskip, gnorm, wout)


def _rope(t, cos, sin_up, sin_dn):
    cols = []
    for c in range(t.shape[1] // LANES):
        tc = t[:, c * LANES:(c + 1) * LANES]
        cols.append(tc * cos + pltpu.roll(tc, ROT_DIM // 2, 1) * sin_up
                    + pltpu.roll(tc, LANES - ROT_DIM // 2, 1) * sin_dn)
    return jnp.concatenate(cols, axis=1)


def _kvq_kernel(h_ref, gkv_ref, gb_ref, wkv_ref, wb_ref, cos_ref, sup_ref, sdn_ref, *out_refs, for_flash):
    xh = _rms(h_ref[...])
    kv = _dot((xh * gkv_ref[...]).astype(BF16), wkv_ref[...])
    qg = _dot((xh * gb_ref[...]).astype(BF16), wb_ref[...])
    cos, sup, sdn = cos_ref[...], sup_ref[...], sdn_ref[...]
    k = _rope(kv[:, :K_DIM], cos, sup, sdn)
    v = kv[:, K_DIM:]
    q = _rope(qg[:, :K_DIM], cos, sup, sdn)
    if for_flash:
        kt_ref, v_ref, kb_ref, vt_ref, qt_ref, gate_ref = out_refs
        kt_ref[0] = k.T
        kb_ref[...] = k.astype(BF16)
        vt = v.T.astype(BF16)
        for h in range(ATT_HEADS):
            vt_ref[0, 0, h, :V_DIM, :] = vt[h * V_DIM:(h + 1) * V_DIM]
            vt_ref[0, 0, h, V_DIM:, :] = jnp.ones((VT_ROWS - V_DIM, vt.shape[1]), BF16)
        qt_ref[0] = (q * Q_SCALE_LOG2).T.astype(BF16)
    else:
        k_ref, v_ref, q_ref, gate_ref = out_refs
        k_ref[...] = k
        q_ref[...] = q
    v_ref[...] = v
    gate_ref[...] = qg[:, K_DIM:]


def _kvq_in_specs(tm, wkv, wb, table_blocks):
    table = pl.BlockSpec((tm, LANES), lambda i: (i % table_blocks, 0))
    return [pl.BlockSpec((tm, D_MODEL), lambda i: (i, 0)), _full((1, D_MODEL)), _full((1, D_MODEL)),
            _full(wkv.shape), _full(wb.shape), table, table, table]


def _kvq(h, gkv, gb, wkv, wb, cos, sup, sdn, tm):
    m = h.shape[0]
    row = pl.BlockSpec((tm, D_MODEL), lambda i: (i, 0))
    out = jax.ShapeDtypeStruct((m, D_MODEL), F32)
    return pl.pallas_call(
        functools.partial(_kvq_kernel, for_flash=False),
        grid=(m // tm,),
        in_specs=_kvq_in_specs(tm, wkv, wb, cos.shape[0] // tm),
        out_specs=[row] * 4,
        out_shape=[out] * 4,
        compiler_params=_params("arbitrary"),
        name="kvq_proj",
    )(h, gkv, gb, wkv, wb, cos, sup, sdn)


def _kvq_flash(h, gkv, gb, wkv, wb, cos, sup, sdn, b, l):
    tm = FLASH_TK
    nb = l // tm
    row = pl.BlockSpec((tm, D_MODEL), lambda i: (i, 0))
    feat = pl.BlockSpec((1, D_MODEL, tm), lambda i: (i // nb, 0, i % nb))
    return pl.pallas_call(
        functools.partial(_kvq_kernel, for_flash=True),
        grid=(b * nb,),
        in_specs=_kvq_in_specs(tm, wkv, wb, nb),
        out_specs=[feat, row, row,
                   pl.BlockSpec((1, 1, ATT_HEADS, VT_ROWS, tm), lambda i: (i // nb, i % nb, 0, 0, 0)), feat, row],
        out_shape=[jax.ShapeDtypeStruct((b, K_DIM, l), F32),
                   jax.ShapeDtypeStruct((b * l, VAL_WIDTH), F32),
                   jax.ShapeDtypeStruct((b * l, K_DIM), BF16),
                   jax.ShapeDtypeStruct((b, nb, ATT_HEADS, VT_ROWS, tm), BF16),
                   jax.ShapeDtypeStruct((b, K_DIM, l), BF16),
                   jax.ShapeDtypeStruct((b * l, VAL_WIDTH), F32)],
        compiler_params=_params("arbitrary"),
        name="kvq_proj_flash",
    )(h, gkv, gb, wkv, wb, cos, sup, sdn)


def _rope_tables(pos):
    inv_freq = ROPE_THETA ** (-jnp.arange(0, ROT_DIM, 2, dtype=F32) / ROT_DIM)
    ang = pos[:, None] * inv_freq[None, :]
    cos, sin = jnp.cos(ang), jnp.sin(ang)
    half = ROT_DIM // 2
    rows = pos.shape[0]
    pad = jnp.zeros((rows, HEAD_DIM - ROT_DIM), F32)
    zero = jnp.zeros((rows, half), F32)
    cos64 = jnp.concatenate([cos, cos, pad + 1.0], axis=1)
    sup64 = jnp.concatenate([zero, sin, pad], axis=1)
    sdn64 = jnp.concatenate([-sin, zero, pad], axis=1)
    return tuple(jnp.tile(t, (1, LANES // HEAD_DIM)) for t in (cos64, sup64, sdn64))


def _lambda(lq1_ref, lk1_ref, lq2_ref, lk2_ref):
    s1 = jnp.sum(lq1_ref[...] * lk1_ref[...], axis=1, keepdims=True)
    s2 = jnp.sum(lq2_ref[...] * lk2_ref[...], axis=1, keepdims=True)
    return jnp.exp(s1) - jnp.exp(s2) + LAMBDA_INIT


def _sub_norm(o, subln):
    return _rms(o) * subln * (1.0 - LAMBDA_INIT)


PAGES_PER_STEP = 8
N_MAPS = 2 * ATT_HEADS


def _attention_kernel(pt_ref, qt_ref, k_ref, vt_ref, lq1_ref, lk1_ref, lq2_ref, lk2_ref, subln_col_ref,
                      subln_row_ref, qs_ref, kn_ref, vn_ref, *rest, tq, tk, groups):
    kt_refs = rest[:PAGES_PER_STEP]
    v_refs = rest[PAGES_PER_STEP:2 * PAGES_PER_STEP]
    o_ref, os_ref, acc_ref, sa_ref, sb_ref, qrow_ref, ms_ref, ls_ref, accs_ref, ps_ref = rest[2 * PAGES_PER_STEP:]
    qi = pl.program_id(2)
    step = (pl.program_id(0) * pl.num_programs(1) + pl.program_id(1)) * pl.num_programs(2) + qi
    group = step % groups
    n = 2 * tq
    half = PAGES_PER_STEP // 2

    @pl.when(group == 0)
    def _():
        map_i = lax.broadcasted_iota(jnp.int32, (N_MAPS, K_DIM), 0)
        seg = 2 * (map_i % ATT_HEADS) + map_i // ATT_HEADS
        lane_seg = lax.broadcasted_iota(jnp.int32, (N_MAPS, K_DIM), 1) // HEAD_DIM
        qs = qs_ref[0] * (HEAD_DIM ** -0.5)
        qrow = jnp.where(lane_seg == seg, jnp.broadcast_to(qs, (N_MAPS, K_DIM)), 0.0)
        qrow_ref[...] = qrow.astype(BF16)
        ms_ref[...] = jnp.sum(qrow * kn_ref[0], axis=1, keepdims=True)
        ls_ref[...] = jnp.ones_like(ls_ref)
        accs_ref[...] = jnp.concatenate([vn_ref[0], vn_ref[0]], axis=0)

    qt = qt_ref[0]
    feat = lax.broadcasted_iota(jnp.int32, (LANES, tq), 0)
    zero = jnp.zeros_like(qt)
    qqt = jnp.concatenate([jnp.where(feat < HEAD_DIM, qt, zero), jnp.where(feat < HEAD_DIM, zero, qt)], axis=1)
    acc_ref[...] = jnp.zeros_like(acc_ref)

    def logits(kb, lanes=slice(None)):
        start = pl.multiple_of(kb * tk, tk)
        return _dot(k_ref[0, pl.ds(start, tk), :], qqt[:, lanes])

    def softmax_pv(st, m, kb, lanes=slice(None)):
        m_new = jnp.maximum(m, jnp.max(st, axis=0, keepdims=True))
        alpha = jnp.exp2(m - m_new)
        p = jnp.exp2(st - m_new).astype(BF16)
        acc_ref[:, lanes] = acc_ref[:, lanes] * alpha + _dot(vt_ref[0, kb, 0], p)
        return m_new

    qrow = qrow_ref[...]
    page_scores = lambda pages: [_dot(qrow, kt_refs[i][0].astype(BF16)) for i in pages]
    s_parts = page_scores(range(half))
    sa_ref[...] = logits(0)
    s_parts += page_scores(range(half, PAGES_PER_STEP))
    s = jnp.concatenate(s_parts, axis=1)
    ms_old = ms_ref[...]
    ms_new = jnp.maximum(ms_old, jnp.max(s, axis=1, keepdims=True))
    alpha_s = jnp.exp(ms_old - ms_new)
    ps = jnp.exp(s - ms_new)
    ls_ref[...] = alpha_s * ls_ref[...] + jnp.sum(ps, axis=1, keepdims=True)
    ms_ref[...] = ms_new
    accs_ref[...] = alpha_s * accs_ref[...]
    ps_ref[...] = ps

    def pair(i, m):
        kb = 2 * i
        sb_ref[...] = logits(kb + 1)
        m = softmax_pv(sa_ref[...], m, kb)
        sa_ref[...] = logits(kb + 2)
        return softmax_pv(sb_ref[...], m, kb + 1)

    m = lax.fori_loop(0, qi, pair, jnp.full((1, n), NEG_INF, F32))

    head_of_map = lax.broadcasted_iota(jnp.int32, (N_MAPS, PAGE_SIZE), 0) % ATT_HEADS

    def page_values(pages):
        pv = jnp.zeros((N_MAPS, V_DIM), F32)
        for i in pages:
            p_i = ps_ref[:, i * PAGE_SIZE:(i + 1) * PAGE_SIZE]
            p_heads = jnp.concatenate([jnp.where(head_of_map == h, p_i, 0.0) for h in range(ATT_HEADS)], axis=1)
            v_heads = jnp.concatenate([v_refs[i][0, pl.ds(h, PAGE_SIZE, stride=ATT_HEADS), :]
                                       for h in range(ATT_HEADS)], axis=0)
            pv = pv + _dot(p_heads.astype(BF16), v_heads.astype(BF16))
        return pv

    kb = 2 * qi
    late = (slice(tk, tq), slice(tq + tk, n))
    tri = lax.broadcasted_iota(jnp.int32, (tk, tk), 0) <= lax.broadcasted_iota(jnp.int32, (tk, tk), 1)
    late_logits = [jnp.where(tri, logits(kb + 1, lanes), NEG_INF) for lanes in late]
    pv = page_values(range(half))
    key_i = lax.broadcasted_iota(jnp.int32, (tk, n), 0)
    qry_i = lax.broadcasted_iota(jnp.int32, (tk, n), 1) % tq
    m = softmax_pv(jnp.where(key_i <= qry_i, sa_ref[...], NEG_INF), m, kb)
    pv = pv + page_values(range(half, PAGES_PER_STEP))
    accs_ref[...] = accs_ref[...] + pv
    for lanes, st in zip(late, late_logits):
        softmax_pv(st, m[:, lanes], kb + 1, lanes)

    lam = _lambda(lq1_ref, lk1_ref, lq2_ref, lk2_ref)
    on = acc_ref[:V_DIM, :] * (1.0 / acc_ref[V_DIM:V_DIM + 1, :])
    ot = on[:, :tq] - lam * on[:, tq:]
    ot = ot * lax.rsqrt(jnp.mean(ot * ot, axis=0, keepdims=True) + EPS) * subln_col_ref[...] * (1.0 - LAMBDA_INIT)
    o_ref[0] = ot.T

    @pl.when(group == groups - 1)
    def _():
        ons = accs_ref[...] / ls_ref[...]
        os_ref[0] = _sub_norm(ons[:ATT_HEADS] - lam * ons[ATT_HEADS:], subln_row_ref[...])


def _attention(qt, kb, vt, page_table, qs, kn, vn, cache_kt, cache_v, lq1, lk1, lq2, lk2, subln, tq):
    b, _, l = qt.shape
    m = qs.shape[0]
    tk = FLASH_TK
    nq = l // tq
    groups = N_PAGES // PAGES_PER_STEP
    assert tq == 2 * tk
    assert b * ATT_HEADS * nq == m * groups

    def step(i, h, t):
        return (i * ATT_HEADS + h) * nq + t

    small = pl.BlockSpec((1, HEAD_DIM), lambda i, h, t, pt: (0, 0))
    row = pl.BlockSpec((1, 1, D_MODEL), lambda i, h, t, pt: (step(i, h, t) // groups, 0, 0))
    heads = pl.BlockSpec((1, ATT_HEADS, V_DIM), lambda i, h, t, pt: (step(i, h, t) // groups, 0, 0))

    def page(r):
        def index(i, h, t, pt):
            s = step(i, h, t)
            return (pt[(s // groups) * N_PAGES + (s % groups) * PAGES_PER_STEP + r], 0, 0)
        return pl.BlockSpec((1, K_DIM, PAGE_SIZE), index)

    pages = [page(r) for r in range(PAGES_PER_STEP)]
    grid_spec = pltpu.PrefetchScalarGridSpec(
        num_scalar_prefetch=1,
        grid=(b, ATT_HEADS, nq),
        in_specs=[pl.BlockSpec((1, LANES, tq), lambda i, h, t, pt: (i, h, t)),
                  pl.BlockSpec((1, l, LANES), lambda i, h, t, pt: (i, 0, h)),
                  pl.BlockSpec((1, l // tk, 1, VT_ROWS, tk), lambda i, h, t, pt: (i, 0, h, 0, 0)),
                  small, small, small, small,
                  pl.BlockSpec((V_DIM, 1), lambda i, h, t, pt: (0, 0)),
                  pl.BlockSpec((1, V_DIM), lambda i, h, t, pt: (0, 0)),
                  row, row, heads] + pages + pages,
        out_specs=[pl.BlockSpec((1, tq, LANES), lambda i, h, t, pt: (i, t, h)), heads],
        scratch_shapes=[pltpu.VMEM((VT_ROWS, 2 * tq), F32),
                        pltpu.VMEM((tk, 2 * tq), F32),
                        pltpu.VMEM((tk, 2 * tq), F32),
                        pltpu.VMEM((N_MAPS, K_DIM), BF16),
                        pltpu.VMEM((N_MAPS, 1), F32),
                        pltpu.VMEM((N_MAPS, 1), F32),
                        pltpu.VMEM((N_MAPS, V_DIM), F32),
                        pltpu.VMEM((N_MAPS, PAGES_PER_STEP * PAGE_SIZE), F32)],
    )
    o, o_s = pl.pallas_call(
        functools.partial(_attention_kernel, tq=tq, tk=tk, groups=groups),
        grid_spec=grid_spec,
        out_shape=[jax.ShapeDtypeStruct((b, l, VAL_WIDTH), F32),
                   jax.ShapeDtypeStruct((m, ATT_HEADS, V_DIM), F32)],
        compiler_params=_params("arbitrary", "arbitrary", "arbitrary"),
        name="attention",
    )(page_table.reshape(-1), qt, kb, vt, lq1, lk1, lq2, lk2, subln.reshape(V_DIM, 1), subln,
      qs.reshape(m, 1, K_DIM), kn.reshape(m, 1, K_DIM), vn.reshape(m, ATT_HEADS, V_DIM),
      *([cache_kt] * PAGES_PER_STEP), *([cache_v] * PAGES_PER_STEP))
    return o, o_s.reshape(m, VAL_WIDTH)


def _attn_out_kernel(o_ref, gate_ref, h_ref, wout_ref, gf_ref, y_ref):
    og = (o_ref[...] * _silu(gate_ref[...])).astype(BF16)
    y_ref[...] = _rms(h_ref[...] + _dot(og, wout_ref[...])) * gf_ref[...]


def _attn_out(o, gate, h, wout, gf, tm):
    m = o.shape[0]
    row = pl.BlockSpec((tm, D_MODEL), lambda i: (i, 0))
    return pl.pallas_call(
        _attn_out_kernel,
        grid=(m // tm,),
        in_specs=[row, row, row, _full(wout.shape), _full((1, D_MODEL))],
        out_specs=row,
        out_shape=jax.ShapeDtypeStruct((m, D_MODEL), F32),
        compiler_params=_params("arbitrary"),
        name="attn_out",
    )(o, gate, h, wout, gf)


def _pad_lanes(v):
    return jnp.pad(v.reshape(1, -1), ((0, 0), (0, LANES - v.shape[-1])))


def kernel(x_prompt, x_sample, cache_k, cache_v, page_table, state_conv, state_ssm, norm_a, w_in_a, conv_w, conv_b, dt_bias, a_log, d_skip, gnorm_a, w_out_a, norm_kv, w_kv, norm_b, w_in_b, lambda_q1, lambda_k1, lambda_q2, lambda_k2, subln_b, w_out_b, norm_f):
    bp, lp, _ = x_prompt.shape
    bs = x_sample.shape[0]
    mp = bp * lp

    w_a = w_in_a[0].astype(BF16)
    wz, wx = w_a[:, :D_INNER], w_a[:, D_INNER:D_INNER + CONV_DIM]
    wdt = jnp.pad(w_a[:, D_INNER + CONV_DIM:], ((0, 0), (0, LANES - SSM_HEADS)))
    wout_a = w_out_a[0].astype(BF16)
    wkv = w_kv.astype(BF16)
    wb = w_in_b[0].astype(BF16)
    wout_b = w_out_b[0].astype(BF16)
    g_a = norm_a[0].reshape(1, D_MODEL)
    convw, convb = conv_w[0], conv_b[0].reshape(1, CONV_DIM)
    dtb, alog = _pad_lanes(dt_bias[0]), _pad_lanes(a_log[0])
    dskip = jnp.repeat(d_skip[0], SSM_HEAD_DIM).reshape(1, D_INNER)
    gnorm = gnorm_a[0].reshape(1, D_INNER)
    g_kv, g_b, g_f = norm_kv.reshape(1, D_MODEL), norm_b[0].reshape(1, D_MODEL), norm_f.reshape(1, D_MODEL)
    lq1, lk1, lq2, lk2 = (t[0].reshape(1, HEAD_DIM) for t in (lambda_q1, lambda_k1, lambda_q2, lambda_k2))
    subln = subln_b[0].reshape(1, V_DIM)
    head_of_lane = jnp.arange(D_INNER) // SSM_HEAD_DIM
    hexp = (jnp.arange(LANES)[:, None] == head_of_lane[None, :]).astype(F32)
    gexp = (jnp.arange(N_GROUPS)[:, None] == (head_of_lane // (SSM_HEADS // N_GROUPS))[None, :]).astype(F32)

    xp = x_prompt.reshape(mp, D_MODEL)
    h1, conv_p, ssm_p = _ssd_prompt(xp, bp, g_a, wz, wx, wdt, convw, convb, dtb, alog, dskip, gnorm, wout_a)
    tables_p = _rope_tables(jnp.arange(lp, dtype=F32))
    kt_p, v_p, kb_p, vt_p, qt_p, gate_p = _kvq_flash(h1, g_kv, g_b, wkv, wb, *tables_p, b=bp, l=lp)
    k_p = jnp.transpose(kt_p.reshape(bp, ATT_HEADS, 2, HEAD_DIM, lp), (0, 4, 1, 2, 3))

    n_pool = cache_k.shape[0]
    cache_kt = jnp.transpose(cache_k, (0, 2, 3, 4, 1)).reshape(n_pool, K_DIM, PAGE_SIZE)
    cache_vr = cache_v.reshape(n_pool, PAGE_SIZE * ATT_HEADS, V_DIM)
    sconv = jnp.transpose(state_conv[0], (1, 0, 2))
    xs_in = x_sample.reshape(bs, D_MODEL)
    z_s, xbc_s, dt_s = _inproj_a(xs_in, g_a, wz, wx, wdt, tm=bs)
    conv_s, xdt, dec, bm, cm, ydiag, xs_act = _ssd_step_pre(xbc_s, dt_s, sconv, convw, convb, dtb, alog, hexp, gexp)
    yoff, ssm_s = _ssd_step_state(xdt, dec, bm, cm, state_ssm[0])
    h1_s = _ssd_step_out(ydiag, yoff, dec, xs_act, z_s, xs_in, dskip, gnorm, wout_a)
    tables_s = _rope_tables(jnp.full((bs,), float(PAST_LEN), F32))
    k_s, v_s, q_s, gate_s = _kvq(h1_s, g_kv, g_b, wkv, wb, *tables_s, tm=bs)

    o_p, o_s = _attention(qt_p, kb_p.reshape(bp, lp, K_DIM), vt_p, page_table, q_s, k_s, v_s, cache_kt, cache_vr,
                          lq1, lk1, lq2, lk2, subln, tq=512)
    y_p = _attn_out(o_p.reshape(mp, VAL_WIDTH), gate_p, h1, wout_b, g_f, tm=256)
    y_s = _attn_out(o_s, gate_s, h1_s, wout_b, g_f, tm=bs)

    return (y_p.reshape(bp, lp, D_MODEL), y_s.reshape(bs, 1, D_MODEL),
            k_p, v_p.reshape(bp, lp, ATT_HEADS, V_DIM),
            conv_p[None], ssm_p[None],
            k_s.reshape(bs, 1, ATT_HEADS, 2, HEAD_DIM), v_s.reshape(bs, 1, ATT_HEADS, V_DIM),
            jnp.transpose(conv_s, (1, 0, 2))[None], ssm_s[None])
```

```python
import functools
import math

import jax
import jax.numpy as jnp
from jax import lax
from jax.experimental import pallas as pl
from jax.experimental.pallas import tpu as pltpu

F32 = jnp.float32
BF16 = jnp.bfloat16
HIGHEST = lax.Precision.HIGHEST

D_MODEL = 1024
PAST_LEN = 8192
PAGE_SIZE = 128
N_PAGES = PAST_LEN // PAGE_SIZE
D_INNER = 2048
SSM_HEAD_DIM = 64
SSM_HEADS = 32
N_GROUPS = 4
D_STATE = 128
D_CONV = 4
CONV_DIM = D_INNER + 2 * N_GROUPS * D_STATE
CHUNK = 128
ATT_HEADS = 8
HEAD_DIM = 64
V_DIM = 128
K_DIM = 1024
VAL_WIDTH = 1024
ROT_DIM = 16
ROPE_THETA = 500000.0
EPS = 1e-6
NEG_INF = -1e30
LAMBDA_INIT = 0.8 - 0.6 * math.exp(-0.3 * 1)

FLASH_TK = 256
VT_ROWS = V_DIM + 16
Q_SCALE_LOG2 = HEAD_DIM ** -0.5 * math.log2(math.e)

LANES = 128
VMEM_LIMIT = 56 * 1024 * 1024

NT_DIMS = (((1,), (1,)), ((), ()))
TN_DIMS = (((0,), (0,)), ((), ()))


def _params(*sem):
    return pltpu.CompilerParams(dimension_semantics=sem, vmem_limit_bytes=VMEM_LIMIT)


def _silu(x):
    return x * (1.0 / (1.0 + jnp.exp(-x)))


def _softplus(x):
    return jnp.maximum(x, 0.0) + jnp.log1p(jnp.exp(-jnp.abs(x)))


def _rms(x):
    return x * lax.rsqrt(jnp.mean(x * x, axis=-1, keepdims=True) + EPS)


def _dot(a, b):
    return jnp.dot(a, b, preferred_element_type=F32)


def _full(shape):
    return pl.BlockSpec(shape, lambda *_: (0,) * len(shape))


def _inproj_a_kernel(x_ref, g_ref, wz_ref, wx_ref, wdt_ref, z_ref, xbc_ref, dt_ref):
    xn = (_rms(x_ref[...]) * g_ref[...]).astype(BF16)
    z_ref[...] = _dot(xn, wz_ref[...])
    xbc_ref[...] = _dot(xn, wx_ref[...])
    dt_ref[...] = _dot(xn, wdt_ref[...])


def _inproj_a(x, g, wz, wx, wdt, tm):
    m = x.shape[0]
    row = lambda n: pl.BlockSpec((tm, n), lambda i: (i, 0))
    return pl.pallas_call(
        _inproj_a_kernel,
        grid=(m // tm,),
        in_specs=[row(D_MODEL), _full((1, D_MODEL)), _full(wz.shape), _full(wx.shape), _full(wdt.shape)],
        out_specs=[row(D_INNER), row(CONV_DIM), row(LANES)],
        out_shape=[jax.ShapeDtypeStruct((m, D_INNER), F32),
                   jax.ShapeDtypeStruct((m, CONV_DIM), F32),
                   jax.ShapeDtypeStruct((m, LANES), F32)],
        compiler_params=_params("arbitrary"),
        name="inproj_a",
    )(x, g, wz, wx, wdt)


def _gate_norm_outproj(y, gate, gnorm, wout, xres):
    yg = y * gate
    gw = D_INNER // N_GROUPS
    parts = [_rms(yg[:, g * gw:(g + 1) * gw]) for g in range(N_GROUPS)]
    yn = (jnp.concatenate(parts, axis=1) * gnorm).astype(BF16)
    return xres + _dot(yn, wout)


def _project_chunk(x, g_ref, wz_ref, wx_ref, wdt_ref, proj):
    z_ref, xbc_ref, dt_ref = proj
    xn = (_rms(x) * g_ref[...]).astype(BF16)
    z_ref[...] = _dot(xn, wz_ref[...])
    xbc_ref[...] = _dot(xn, wx_ref[...])
    dt_ref[...] = _dot(xn, wdt_ref[...])


PROJ_COLS = 512


def _scan_chunk(proj, xres, convw_ref, convb_ref, dtb_ref, alog_ref, dskip_ref, gnorm_ref, wout_ref,
                xpad_ref, st_ref, work, x_ahead, g_ref, wz_ref, wx_ref, wdt_ref, proj_ahead):
    z_ref, xbc_ref, dt_ref = proj
    za_ref, xbca_ref, dta_ref = proj_ahead
    act_ref, gate_ref, y_ref = work
    q = CHUNK
    xn = (_rms(x_ahead) * g_ref[...]).astype(BF16)
    sub8 = lax.broadcasted_iota(jnp.int32, (q // 8, 8, PROJ_COLS), 1)

    for blk in range(CONV_DIM // PROJ_COLS):
        cols = slice(blk * PROJ_COLS, (blk + 1) * PROJ_COLS)
        xbca_ref[:, cols] = _dot(xn, wx_ref[:, cols])
        xbc = xbc_ref[:, cols]
        xpad_ref[8:8 + q, cols] = xbc
        conv = (convb_ref[:, cols] + xbc * convw_ref[3:4, cols]).reshape(q // 8, 8, PROJ_COLS)
        groups = xpad_ref[:, cols].reshape(q // 8 + 1, 8, PROJ_COLS)
        for k in range(1, D_CONV):
            rot = pltpu.roll(groups, k, 1)
            back_k = jnp.where(sub8 < k, rot[:q // 8], rot[1:])
            conv = conv + back_k * convw_ref[D_CONV - 1 - k:D_CONV - k, cols]
        act_ref[:, cols] = _silu(conv).reshape(q, PROJ_COLS)
        xpad_ref[0:8, cols] = xpad_ref[q:q + 8, cols]
    for blk in range(D_INNER // PROJ_COLS):
        cols = slice(blk * PROJ_COLS, (blk + 1) * PROJ_COLS)
        za_ref[:, cols] = _dot(xn, wz_ref[:, cols])
        gate_ref[:, cols] = _silu(z_ref[:, cols])
    dta_ref[...] = _dot(xn, wdt_ref[...])

    dt = _softplus(dt_ref[...] + dtb_ref[...])
    a = -jnp.exp(alog_ref[...])
    row_i = lax.broadcasted_iota(jnp.int32, (q, q), 0)
    col_i = lax.broadcasted_iota(jnp.int32, (q, q), 1)
    causal = row_i >= col_i
    a_cs = jnp.dot(causal.astype(F32), dt * a, precision=HIGHEST, preferred_element_type=F32)
    a_cs_t = a_cs.T
    dt_t = dt.T
    w_t = dt_t * jnp.exp(a_cs_t[:, q - 1:q] - a_cs_t)
    e_cs = jnp.exp(a_cs)
    first_half = col_i < SSM_HEAD_DIM

    for g in range(N_GROUPS):
        bm = act_ref[:, D_INNER + g * D_STATE:D_INNER + (g + 1) * D_STATE]
        cm = act_ref[:, D_INNER + (N_GROUPS + g) * D_STATE:D_INNER + (N_GROUPS + g + 1) * D_STATE]
        bm_b = bm.astype(BF16)
        cm_b = cm.astype(BF16)
        cb = lax.dot_general(cm_b, bm_b, NT_DIMS, preferred_element_type=F32)
        bm_t = bm.T
        for jj in range(SSM_HEADS // N_GROUPS // 2):
            j = g * (SSM_HEADS // N_GROUPS // 2) + jj
            xs = act_ref[:, j * LANES:(j + 1) * LANES]
            xs_b = xs.astype(BF16)
            zero = jnp.zeros_like(xs_b)
            xbd = jnp.concatenate([jnp.where(first_half, xs_b, zero),
                                   jnp.where(first_half, zero, xs_b)], axis=0)
            m_parts, w_parts = [], []
            for h in (2 * j, 2 * j + 1):
                seg = a_cs[:, h:h + 1] - a_cs_t[h:h + 1, :]
                decay = jnp.exp(jnp.where(causal, seg, -jnp.inf))
                m_parts.append((cb * decay * dt_t[h:h + 1, :]).astype(BF16))
                w_parts.append((bm_t * w_t[h:h + 1, :]).astype(BF16))
            y_diag = _dot(jnp.concatenate(m_parts, axis=1), xbd)
            upd_t = _dot(jnp.concatenate(w_parts, axis=1), xbd)
            st = st_ref[j]
            e_sel = jnp.where(first_half, e_cs[:, 2 * j:2 * j + 1], e_cs[:, 2 * j + 1:2 * j + 2])
            y_off = _dot(cm_b, st.astype(BF16)) * e_sel
            st_ref[j] = st * e_sel[q - 1:q, :] + upd_t
            y_ref[:, j * LANES:(j + 1) * LANES] = y_diag + y_off + dskip_ref[:, j * LANES:(j + 1) * LANES] * xs

    return _gate_norm_outproj(y_ref[...], gate_ref[...], gnorm_ref[...], wout_ref[...], xres)


def _ssd_prompt_kernel(x2_ref, xnext_ref, g_ref, wz_ref, wx_ref, wdt_ref, convw_ref, convb_ref, dtb_ref, alog_ref,
                       dskip_ref, gnorm_ref, wout_ref,
                       h_ref, conv_out_ref, ssm_out_ref,
                       z0_ref, xbc0_ref, dt0_ref, z1_ref, xbc1_ref, dt1_ref,
                       act0_ref, gate0_ref, y0_ref, act1_ref, gate1_ref, y1_ref, xpad_ref, st_ref,
                       *, steps_per_seq):
    i = pl.program_id(0)
    q = CHUNK
    proj0, proj1 = (z0_ref, xbc0_ref, dt0_ref), (z1_ref, xbc1_ref, dt1_ref)
    weights = (g_ref, wz_ref, wx_ref, wdt_ref)
    consts = (convw_ref, convb_ref, dtb_ref, alog_ref, dskip_ref, gnorm_ref, wout_ref)

    @pl.when(i % steps_per_seq == 0)
    def _():
        xpad_ref[0:8, :] = jnp.zeros((8, CONV_DIM), F32)
        st_ref[...] = jnp.zeros_like(st_ref)

    @pl.when(i == 0)
    def _():
        _project_chunk(x2_ref[0:q, :], *weights, proj0)

    h_ref[0:q, :] = _scan_chunk(proj0, x2_ref[0:q, :], *consts, xpad_ref, st_ref, (act0_ref, gate0_ref, y0_ref),
                                x2_ref[q:2 * q, :], *weights, proj1)
    h_ref[q:2 * q, :] = _scan_chunk(proj1, x2_ref[q:2 * q, :], *consts, xpad_ref, st_ref,
                                    (act1_ref, gate1_ref, y1_ref), xnext_ref[...], *weights, proj0)

    @pl.when(i % steps_per_seq == steps_per_seq - 1)
    def _():
        conv_out_ref[0] = xpad_ref[5:8, :]
        for j in range(SSM_HEADS // 2):
            ssm_out_ref[0, 2 * j:2 * j + 2] = st_ref[j].T.reshape(2, SSM_HEAD_DIM, D_STATE)


def _ssd_prompt(x, b, g, wz, wx, wdt, convw, convb, dtb, alog, dskip, gnorm, wout):
    m = x.shape[0]
    steps = m // (2 * CHUNK)
    steps_per_seq = steps // b
    n_chunks = m // CHUNK
    vm = lambda *shape: pltpu.VMEM(shape, F32)
    per_seq = lambda *blk: pl.BlockSpec((1,) + blk, lambda i: (i // steps_per_seq,) + (0,) * len(blk))
    return pl.pallas_call(
        functools.partial(_ssd_prompt_kernel, steps_per_seq=steps_per_seq),
        grid=(steps,),
        in_specs=[pl.BlockSpec((2 * CHUNK, D_MODEL), lambda i: (i, 0)),
                  pl.BlockSpec((CHUNK, D_MODEL), lambda i: (jnp.minimum(2 * i + 2, n_chunks - 1), 0)),
                  _full(g.shape), _full(wz.shape), _full(wx.shape), _full(wdt.shape),
                  _full(convw.shape), _full(convb.shape), _full(dtb.shape), _full(alog.shape),
                  _full(dskip.shape), _full(gnorm.shape), _full(wout.shape)],
        out_specs=[pl.BlockSpec((2 * CHUNK, D_MODEL), lambda i: (i, 0)),
                   per_seq(D_CONV - 1, CONV_DIM),
                   per_seq(SSM_HEADS, SSM_HEAD_DIM, D_STATE)],
        out_shape=[jax.ShapeDtypeStruct((m, D_MODEL), F32),
                   jax.ShapeDtypeStruct((b, D_CONV - 1, CONV_DIM), F32),
                   jax.ShapeDtypeStruct((b, SSM_HEADS, SSM_HEAD_DIM, D_STATE), F32)],
        scratch_shapes=[vm(CHUNK, D_INNER), vm(CHUNK, CONV_DIM), vm(CHUNK, LANES),
                        vm(CHUNK, D_INNER), vm(CHUNK, CONV_DIM), vm(CHUNK, LANES),
                        vm(CHUNK, CONV_DIM), vm(CHUNK, D_INNER), vm(CHUNK, D_INNER),
                        vm(CHUNK, CONV_DIM), vm(CHUNK, D_INNER), vm(CHUNK, D_INNER),
                        vm(CHUNK + 8, CONV_DIM),
                        vm(SSM_HEADS // 2, D_STATE, LANES)],
        compiler_params=_params("arbitrary"),
        name="ssd_prompt",
    )(x, x, g, wz, wx, wdt, convw, convb, dtb, alog, dskip, gnorm, wout)


def _ssd_step_pre_kernel(xbc_ref, dt_ref, sconv_ref, convw_ref, convb_ref, dtb_ref, alog_ref,
                         hexp_ref, gexp_ref,
                         conv_out_ref, xdt_ref, dec_ref, bm_ref, cm_ref, ydiag_ref, xs_ref):
    xbc = xbc_ref[...]
    conv = convb_ref[...] + xbc * convw_ref[3:4, :]
    for k in range(D_CONV - 1):
        conv = conv + sconv_ref[k] * convw_ref[k:k + 1, :]
    conv_out_ref[0] = sconv_ref[1]
    conv_out_ref[1] = sconv_ref[2]
    conv_out_ref[2] = xbc
    act = _silu(conv)
    xs = act[:, :D_INNER]
    bm = act[:, D_INNER:D_INNER + N_GROUPS * D_STATE]
    cm = act[:, D_INNER + N_GROUPS * D_STATE:]
    dt = _softplus(dt_ref[...] + dtb_ref[...])
    dec = jnp.exp(dt * -jnp.exp(alog_ref[...]))
    hexp = hexp_ref[...]
    dt_x = jnp.dot(dt, hexp, precision=HIGHEST, preferred_element_type=F32)
    dec_x = jnp.dot(dec, hexp, precision=HIGHEST, preferred_element_type=F32)
    prod = bm * cm
    cb = jnp.concatenate([jnp.sum(prod[:, g * D_STATE:(g + 1) * D_STATE], axis=1, keepdims=True)
                          for g in range(N_GROUPS)], axis=1)
    cb_x = jnp.dot(cb, gexp_ref[...], precision=HIGHEST, preferred_element_type=F32)
    xdt = xs * dt_x
    xdt_ref[...] = xdt
    dec_ref[...] = dec_x
    bm_ref[...] = bm
    cm_ref[...] = cm
    ydiag_ref[...] = cb_x * xdt
    xs_ref[...] = xs


def _ssd_step_pre(xbc, dt, sconv, convw, convb, dtb, alog, hexp, gexp):
    m = xbc.shape[0]
    gs = N_GROUPS * D_STATE
    return pl.pallas_call(
        _ssd_step_pre_kernel,
        out_shape=[jax.ShapeDtypeStruct((D_CONV - 1, m, CONV_DIM), F32),
                   jax.ShapeDtypeStruct((m, D_INNER), F32),
                   jax.ShapeDtypeStruct((m, D_INNER), F32),
                   jax.ShapeDtypeStruct((m, gs), F32),
                   jax.ShapeDtypeStruct((m, gs), F32),
                   jax.ShapeDtypeStruct((m, D_INNER), F32),
                   jax.ShapeDtypeStruct((m, D_INNER), F32)],
        compiler_params=pltpu.CompilerParams(vmem_limit_bytes=VMEM_LIMIT),
        name="ssd_step_pre",
    )(xbc, dt, sconv, convw, convb, dtb, alog, hexp, gexp)


def _ssd_step_state_kernel(xdt_ref, dec_ref, bm_ref, cm_ref, st_ref, yoff_ref, st_out_ref):
    gw = D_INNER // N_GROUPS
    hg = SSM_HEADS // N_GROUPS
    sub = lax.broadcasted_iota(jnp.int32, (8, gw), 0)
    sub_n = lax.broadcasted_iota(jnp.int32, (8, D_STATE), 0)
    ones_row = jnp.where(sub_n == 0, 1.0, 0.0).astype(F32)
    for g in range(N_GROUPS):
        xdt8 = jnp.where(sub == 0, xdt_ref[0, :, g * gw:(g + 1) * gw], 0.0)
        dec8 = jnp.where(sub == 0, dec_ref[0, :, g * gw:(g + 1) * gw], 0.0)
        bm8 = jnp.where(sub_n == 0, bm_ref[0, :, g * D_STATE:(g + 1) * D_STATE], 0.0)
        cm8 = jnp.broadcast_to(cm_ref[0, :, g * D_STATE:(g + 1) * D_STATE], (8, D_STATE)).astype(BF16)
        st = st_ref[0, g * hg:(g + 1) * hg].reshape(gw, D_STATE)
        y8 = lax.dot_general(cm8, st.astype(BF16), NT_DIMS, preferred_element_type=F32)
        yoff_ref[0, :, g * gw:(g + 1) * gw] = y8[0:1, :]
        dec_col = lax.dot_general(dec8, ones_row, TN_DIMS, precision=HIGHEST, preferred_element_type=F32)
        upd = lax.dot_general(xdt8, bm8, TN_DIMS, precision=HIGHEST, preferred_element_type=F32)
        st_out_ref[0, g * hg:(g + 1) * hg] = (st * dec_col + upd).reshape(hg, SSM_HEAD_DIM, D_STATE)


def _ssd_step_state(xdt, dec, bm, cm, st):
    m = xdt.shape[0]
    gs = N_GROUPS * D_STATE
    row = lambda n: pl.BlockSpec((1, 1, n), lambda i: (i, 0, 0))
    st_spec = pl.BlockSpec((1, SSM_HEADS, SSM_HEAD_DIM, D_STATE), lambda i: (i, 0, 0, 0))
    yoff, st_out = pl.pallas_call(
        _ssd_step_state_kernel,
        grid=(m,),
        in_specs=[row(D_INNER), row(D_INNER), row(gs), row(gs), st_spec],
        out_specs=[row(D_INNER), st_spec],
        out_shape=[jax.ShapeDtypeStruct((m, 1, D_INNER), F32),
                   jax.ShapeDtypeStruct(st.shape, F32)],
        compiler_params=_params("arbitrary"),
        name="ssd_step_state",
    )(xdt.reshape(m, 1, D_INNER), dec.reshape(m, 1, D_INNER), bm.reshape(m, 1, gs), cm.reshape(m, 1, gs), st)
    return yoff.reshape(m, D_INNER), st_out


def _ssd_step_out_kernel(ydiag_ref, yoff_ref, dec_ref, xs_ref, z_ref, x_ref, dskip_ref, gnorm_ref, wout_ref,
                         h_ref):
    y = ydiag_ref[...] + yoff_ref[...] * dec_ref[...] + dskip_ref[...] * xs_ref[...]
    h_ref[...] = _gate_norm_outproj(y, _silu(z_ref[...]), gnorm_ref[...], wout_ref[...], x_ref[...])


def _ssd_step_out(ydiag, yoff, dec, xs, z, x, dskip, gnorm, wout):
    return pl.pallas_call(
        _ssd_step_out_kernel,
        out_shape=jax.ShapeDtypeStruct(x.shape, F32),
        compiler_params=pltpu.CompilerParams(vmem_limit_bytes=VMEM_LIMIT),
        name="ssd_step_out",
    )(ydiag, yoff, dec, xs, z, x, dskip, gnorm, wout)


def _rope(t, cos, sin_up, sin_dn):
    cols = []
    for c in range(t.shape[1] // LANES):
        tc = t[:, c * LANES:(c + 1) * LANES]
        cols.append(tc * cos + pltpu.roll(tc, ROT_DIM // 2, 1) * sin_up
                    + pltpu.roll(tc, LANES - ROT_DIM // 2, 1) * sin_dn)
    return jnp.concatenate(cols, axis=1)


def _kvq_kernel(h_ref, gkv_ref, gb_ref, wkv_ref, wb_ref, cos_ref, sup_ref, sdn_ref, *out_refs, for_flash):
    xh = _rms(h_ref[...])
    kv = _dot((xh * gkv_ref[...]).astype(BF16), wkv_ref[...])
    qg = _dot((xh * gb_ref[...]).astype(BF16), wb_ref[...])
    cos, sup, sdn = cos_ref[...], sup_ref[...], sdn_ref[...]
    k = _rope(kv[:, :K_DIM], cos, sup, sdn)
    v = kv[:, K_DIM:]
    q = _rope(qg[:, :K_DIM], cos, sup, sdn)
    if for_flash:
        kt_ref, v_ref, kb_ref, vt_ref, qt_ref, gate_ref = out_refs
        kt_ref[0] = k.T
        kb_ref[...] = k.astype(BF16)
        vt = v.T.astype(BF16)
        for h in range(ATT_HEADS):
            vt_ref[0, 0, h, :V_DIM, :] = vt[h * V_DIM:(h + 1) * V_DIM]
            vt_ref[0, 0, h, V_DIM:, :] = jnp.ones((VT_ROWS - V_DIM, vt.shape[1]), BF16)
        qt_ref[0] = (q * Q_SCALE_LOG2).T.astype(BF16)
    else:
        k_ref, v_ref, q_ref, gate_ref = out_refs
        k_ref[...] = k
        q_ref[...] = q
    v_ref[...] = v
    gate_ref[...] = qg[:, K_DIM:]


def _kvq_in_specs(tm, wkv, wb, table_blocks):
    table = pl.BlockSpec((tm, LANES), lambda i: (i % table_blocks, 0))
    return [pl.BlockSpec((tm, D_MODEL), lambda i: (i, 0)), _full((1, D_MODEL)), _full((1, D_MODEL)),
            _full(wkv.shape), _full(wb.shape), table, table, table]


def _kvq(h, gkv, gb, wkv, wb, cos, sup, sdn, tm):
    m = h.shape[0]
    row = pl.BlockSpec((tm, D_MODEL), lambda i: (i, 0))
    out = jax.ShapeDtypeStruct((m, D_MODEL), F32)
    return pl.pallas_call(
        functools.partial(_kvq_kernel, for_flash=False),
        grid=(m // tm,),
        in_specs=_kvq_in_specs(tm, wkv, wb, cos.shape[0] // tm),
        out_specs=[row] * 4,
        out_shape=[out] * 4,
        compiler_params=_params("arbitrary"),
        name="kvq_proj",
    )(h, gkv, gb, wkv, wb, cos, sup, sdn)


def _kvq_flash(h, gkv, gb, wkv, wb, cos, sup, sdn, b, l):
    tm = FLASH_TK
    nb = l // tm
    row = pl.BlockSpec((tm, D_MODEL), lambda i: (i, 0))
    feat = pl.BlockSpec((1, D_MODEL, tm), lambda i: (i // nb, 0, i % nb))
    return pl.pallas_call(
        functools.partial(_kvq_kernel, for_flash=True),
        grid=(b * nb,),
        in_specs=_kvq_in_specs(tm, wkv, wb, nb),
        out_specs=[feat, row, row,
                   pl.BlockSpec((1, 1, ATT_HEADS, VT_ROWS, tm), lambda i: (i // nb, i % nb, 0, 0, 0)), feat, row],
        out_shape=[jax.ShapeDtypeStruct((b, K_DIM, l), F32),
                   jax.ShapeDtypeStruct((b * l, VAL_WIDTH), F32),
                   jax.ShapeDtypeStruct((b * l, K_DIM), BF16),
                   jax.ShapeDtypeStruct((b, nb, ATT_HEADS, VT_ROWS, tm), BF16),
                   jax.ShapeDtypeStruct((b, K_DIM, l), BF16),
                   jax.ShapeDtypeStruct((b * l, VAL_WIDTH), F32)],
        compiler_params=_params("arbitrary"),
        name="kvq_proj_flash",
    )(h, gkv, gb, wkv, wb, cos, sup, sdn)


def _rope_tables(pos):
    inv_freq = ROPE_THETA ** (-jnp.arange(0, ROT_DIM, 2, dtype=F32) / ROT_DIM)
    ang = pos[:, None] * inv_freq[None, :]
    cos, sin = jnp.cos(ang), jnp.sin(ang)
    half = ROT_DIM // 2
    rows = pos.shape[0]
    pad = jnp.zeros((rows, HEAD_DIM - ROT_DIM), F32)
    zero = jnp.zeros((rows, half), F32)
    cos64 = jnp.concatenate([cos, cos, pad + 1.0], axis=1)
    sup64 = jnp.concatenate([zero, sin, pad], axis=1)
    sdn64 = jnp.concatenate([-sin, zero, pad], axis=1)
    return tuple(jnp.tile(t, (1, LANES // HEAD_DIM)) for t in (cos64, sup64, sdn64))


def _lambda(lq1_ref, lk1_ref, lq2_ref, lk2_ref):
    s1 = jnp.sum(lq1_ref[...] * lk1_ref[...], axis=1, keepdims=True)
    s2 = jnp.sum(lq2_ref[...] * lk2_ref[...], axis=1, keepdims=True)
    return jnp.exp(s1) - jnp.exp(s2) + LAMBDA_INIT


def _sub_norm(o, subln):
    return _rms(o) * subln * (1.0 - LAMBDA_INIT)


PAGES_PER_STEP = 8
N_MAPS = 2 * ATT_HEADS


def _attention_kernel(pt_ref, qt_ref, k_ref, vt_ref, lq1_ref, lk1_ref, lq2_ref, lk2_ref, subln_col_ref,
                      subln_row_ref, spread_ref, qs_ref, kn_ref, vn_ref, *rest, tq, tk, groups):
    kt_refs = rest[:PAGES_PER_STEP]
    v_refs = rest[PAGES_PER_STEP:2 * PAGES_PER_STEP]
    (o_ref, os_ref, acc_ref, sa_ref, sb_ref, qrow_ref, ms_ref, ls_ref, accs_ref, ss_ref,
     ps_ref) = rest[2 * PAGES_PER_STEP:]
    qi = pl.program_id(2)
    step = (pl.program_id(0) * pl.num_programs(1) + pl.program_id(1)) * pl.num_programs(2) + qi
    group = step % groups
    n = 2 * tq
    half = PAGES_PER_STEP // 2

    @pl.when(group == 0)
    def _():
        map_i = lax.broadcasted_iota(jnp.int32, (N_MAPS, K_DIM), 0)
        seg = 2 * (map_i % ATT_HEADS) + map_i // ATT_HEADS
        lane_seg = lax.broadcasted_iota(jnp.int32, (N_MAPS, K_DIM), 1) // HEAD_DIM
        qs = qs_ref[0] * (HEAD_DIM ** -0.5)
        qrow = jnp.where(lane_seg == seg, jnp.broadcast_to(qs, (N_MAPS, K_DIM)), 0.0)
        qrow_ref[...] = qrow.astype(BF16)
        ms_ref[...] = jnp.sum(qrow * kn_ref[0], axis=1, keepdims=True)
        ls_ref[...] = jnp.ones_like(ls_ref)
        accs_ref[...] = jnp.concatenate([vn_ref[0], vn_ref[0]], axis=0)

    qt = qt_ref[0]
    feat = lax.broadcasted_iota(jnp.int32, (LANES, tq), 0)
    zero = jnp.zeros_like(qt)
    qqt = jnp.concatenate([jnp.where(feat < HEAD_DIM, qt, zero), jnp.where(feat < HEAD_DIM, zero, qt)], axis=1)
    acc_ref[...] = jnp.zeros_like(acc_ref)

    def logits(kb, lanes=slice(None)):
        start = pl.multiple_of(kb * tk, tk)
        return _dot(k_ref[0, pl.ds(start, tk), :], qqt[:, lanes])

    def softmax_pv(st, m, kb, lanes=slice(None)):
        m_new = jnp.maximum(m, jnp.max(st, axis=0, keepdims=True))
        alpha = jnp.exp2(m - m_new)
        p = jnp.exp2(st - m_new).astype(BF16)
        acc_ref[:, lanes] = acc_ref[:, lanes] * alpha + _dot(vt_ref[0, kb, 0], p)
        return m_new

    qrow = qrow_ref[...]
    for i in range(0, PAGES_PER_STEP, 2):
        ss_ref[:, i * PAGE_SIZE:(i + 2) * PAGE_SIZE] = _dot(
            qrow, jnp.concatenate([kt_refs[i][0].astype(BF16), kt_refs[i + 1][0].astype(BF16)], axis=1))
        if i == half - 2:
            sa_ref[...] = logits(0)

    def pair(i, m):
        kb = 2 * i
        sb_ref[...] = logits(kb + 1)
        m = softmax_pv(sa_ref[...], m, kb)
        sa_ref[...] = logits(kb + 2)
        return softmax_pv(sb_ref[...], m, kb + 1)

    m = lax.fori_loop(0, qi, pair, jnp.full((1, n), NEG_INF, F32))

    s = ss_ref[...]
    ms_old = ms_ref[...]
    ms_new = jnp.maximum(ms_old, jnp.max(s, axis=1, keepdims=True))
    alpha_s = jnp.exp(ms_old - ms_new)
    ps = jnp.exp(s - ms_new)
    ls_ref[...] = alpha_s * ls_ref[...] + jnp.sum(ps, axis=1, keepdims=True)
    ms_ref[...] = ms_new
    p_rows = jnp.concatenate([ps[:, i * PAGE_SIZE:(i + 1) * PAGE_SIZE] for i in range(PAGES_PER_STEP)], axis=0)
    p_wide = _dot(p_rows.astype(BF16), spread_ref[...])
    own_head = (lax.broadcasted_iota(jnp.int32, p_wide.shape, 0) % ATT_HEADS
                == lax.broadcasted_iota(jnp.int32, p_wide.shape, 1) % ATT_HEADS)
    ps_ref[...] = jnp.where(own_head, p_wide, 0.0).astype(BF16)

    def page_values(pages):
        pv = jnp.zeros((N_MAPS, V_DIM), F32)
        for i in pages[::2]:
            two = _dot(ps_ref[i * N_MAPS:(i + 2) * N_MAPS, :],
                       jnp.concatenate([v_refs[i][0].astype(BF16), v_refs[i + 1][0].astype(BF16)], axis=1))
            pv = pv + two[:N_MAPS, :V_DIM] + two[N_MAPS:, V_DIM:]
        return pv

    kb = 2 * qi
    late = (slice(tk, tq), slice(tq + tk, n))
    tri = lax.broadcasted_iota(jnp.int32, (tk, tk), 0) <= lax.broadcasted_iota(jnp.int32, (tk, tk), 1)
    late_logits = [jnp.where(tri, logits(kb + 1, lanes), NEG_INF) for lanes in late]
    pv = page_values(range(half))
    key_i = lax.broadcasted_iota(jnp.int32, (tk, n), 0)
    qry_i = lax.broadcasted_iota(jnp.int32, (tk, n), 1) % tq
    m = softmax_pv(jnp.where(key_i <= qry_i, sa_ref[...], NEG_INF), m, kb)
    pv = pv + page_values(range(half, PAGES_PER_STEP))
    accs_ref[...] = alpha_s * accs_ref[...] + pv
    for lanes, st in zip(late, late_logits):
        softmax_pv(st, m[:, lanes], kb + 1, lanes)

    lam = _lambda(lq1_ref, lk1_ref, lq2_ref, lk2_ref)
    on = acc_ref[:V_DIM, :] * (1.0 / acc_ref[V_DIM:V_DIM + 1, :])
    ot = on[:, :tq] - lam * on[:, tq:]
    ot = ot * lax.rsqrt(jnp.mean(ot * ot, axis=0, keepdims=True) + EPS) * subln_col_ref[...] * (1.0 - LAMBDA_INIT)
    o_ref[0] = ot.T.astype(o_ref.dtype)

    @pl.when(group == groups - 1)
    def _():
        ons = accs_ref[...] / ls_ref[...]
        os_ref[0] = _sub_norm(ons[:ATT_HEADS] - lam * ons[ATT_HEADS:], subln_row_ref[...])


def _attention(qt, kb, vt, page_table, qs, kn, vn, cache_kt, cache_v, lq1, lk1, lq2, lk2, subln, tq):
    b, _, l = qt.shape
    m = qs.shape[0]
    tk = FLASH_TK
    nq = l // tq
    groups = N_PAGES // PAGES_PER_STEP
    assert tq == 2 * tk
    assert b * ATT_HEADS * nq == m * groups
    spread = (jnp.arange(PAGE_SIZE)[:, None] == jnp.arange(PAGE_SIZE * ATT_HEADS)[None, :] // ATT_HEADS).astype(BF16)

    def step(i, h, t):
        return (i * ATT_HEADS + h) * nq + t

    small = pl.BlockSpec((1, HEAD_DIM), lambda i, h, t, pt: (0, 0))
    row = pl.BlockSpec((1, 1, D_MODEL), lambda i, h, t, pt: (step(i, h, t) // groups, 0, 0))
    heads = pl.BlockSpec((1, ATT_HEADS, V_DIM), lambda i, h, t, pt: (step(i, h, t) // groups, 0, 0))

    def page(r):
        def index(i, h, t, pt):
            s = step(i, h, t)
            return (pt[(s // groups) * N_PAGES + (s % groups) * PAGES_PER_STEP + r], 0, 0)
        return pl.BlockSpec((1, K_DIM, PAGE_SIZE), index)

    pages = [page(r) for r in range(PAGES_PER_STEP)]
    grid_spec = pltpu.PrefetchScalarGridSpec(
        num_scalar_prefetch=1,
        grid=(b, ATT_HEADS, nq),
        in_specs=[pl.BlockSpec((1, LANES, tq), lambda i, h, t, pt: (i, h, t)),
                  pl.BlockSpec((1, l, LANES), lambda i, h, t, pt: (i, 0, h)),
                  pl.BlockSpec((1, l // tk, 1, VT_ROWS, tk), lambda i, h, t, pt: (i, 0, h, 0, 0)),
                  small, small, small, small,
                  pl.BlockSpec((V_DIM, 1), lambda i, h, t, pt: (0, 0)),
                  pl.BlockSpec((1, V_DIM), lambda i, h, t, pt: (0, 0)),
                  pl.BlockSpec(spread.shape, lambda i, h, t, pt: (0, 0)),
                  row, row, heads] + pages + pages,
        out_specs=[pl.BlockSpec((1, tq, LANES), lambda i, h, t, pt: (i, t, h)), heads],
        scratch_shapes=[pltpu.VMEM((VT_ROWS, 2 * tq), F32),
                        pltpu.VMEM((tk, 2 * tq), F32),
                        pltpu.VMEM((tk, 2 * tq), F32),
                        pltpu.VMEM((N_MAPS, K_DIM), BF16),
                        pltpu.VMEM((N_MAPS, 1), F32),
                        pltpu.VMEM((N_MAPS, 1), F32),
                        pltpu.VMEM((N_MAPS, V_DIM), F32),
                        pltpu.VMEM((N_MAPS, PAGES_PER_STEP * PAGE_SIZE), F32),
                        pltpu.VMEM((PAGES_PER_STEP * N_MAPS, PAGE_SIZE * ATT_HEADS), BF16)],
    )
    o, o_s = pl.pallas_call(
        functools.partial(_attention_kernel, tq=tq, tk=tk, groups=groups),
        grid_spec=grid_spec,
        out_shape=[jax.ShapeDtypeStruct((b, l, VAL_WIDTH), BF16),
                   jax.ShapeDtypeStruct((m, ATT_HEADS, V_DIM), F32)],
        compiler_params=_params("arbitrary", "arbitrary", "arbitrary"),
        name="attention",
    )(page_table.reshape(-1), qt, kb, vt, lq1, lk1, lq2, lk2, subln.reshape(V_DIM, 1), subln, spread,
      qs.reshape(m, 1, K_DIM), kn.reshape(m, 1, K_DIM), vn.reshape(m, ATT_HEADS, V_DIM),
      *([cache_kt] * PAGES_PER_STEP), *([cache_v] * PAGES_PER_STEP))
    return o, o_s.reshape(m, VAL_WIDTH)


def _attn_out_kernel(o_ref, gate_ref, h_ref, wout_ref, gf_ref, y_ref):
    og = (o_ref[...] * _silu(gate_ref[...])).astype(BF16)
    y_ref[...] = _rms(h_ref[...] + _dot(og, wout_ref[...])) * gf_ref[...]


def _attn_out(o, gate, h, wout, gf, tm):
    m = o.shape[0]
    row = pl.BlockSpec((tm, D_MODEL), lambda i: (i, 0))
    return pl.pallas_call(
        _attn_out_kernel,
        grid=(m // tm,),
        in_specs=[row, row, row, _full(wout.shape), _full((1, D_MODEL))],
        out_specs=row,
        out_shape=jax.ShapeDtypeStruct((m, D_MODEL), F32),
        compiler_params=_params("arbitrary"),
        name="attn_out",
    )(o, gate, h, wout, gf)


def _pad_lanes(v):
    return jnp.pad(v.reshape(1, -1), ((0, 0), (0, LANES - v.shape[-1])))


def kernel(x_prompt, x_sample, cache_k, cache_v, page_table, state_conv, state_ssm, norm_a, w_in_a, conv_w, conv_b, dt_bias, a_log, d_skip, gnorm_a, w_out_a, norm_kv, w_kv, norm_b, w_in_b, lambda_q1, lambda_k1, lambda_q2, lambda_k2, subln_b, w_out_b, norm_f):
    bp, lp, _ = x_prompt.shape
    bs = x_sample.shape[0]
    mp = bp * lp

    w_a = w_in_a[0].astype(BF16)
    wz, wx = w_a[:, :D_INNER], w_a[:, D_INNER:D_INNER + CONV_DIM]
    wdt = jnp.pad(w_a[:, D_INNER + CONV_DIM:], ((0, 0), (0, LANES - SSM_HEADS)))
    wout_a = w_out_a[0].astype(BF16)
    wkv = w_kv.astype(BF16)
    wb = w_in_b[0].astype(BF16)
    wout_b = w_out_b[0].astype(BF16)
    g_a = norm_a[0].reshape(1, D_MODEL)
    convw, convb = conv_w[0], conv_b[0].reshape(1, CONV_DIM)
    dtb, alog = _pad_lanes(dt_bias[0]), _pad_lanes(a_log[0])
    dskip = jnp.repeat(d_skip[0], SSM_HEAD_DIM).reshape(1, D_INNER)
    gnorm = gnorm_a[0].reshape(1, D_INNER)
    g_kv, g_b, g_f = norm_kv.reshape(1, D_MODEL), norm_b[0].reshape(1, D_MODEL), norm_f.reshape(1, D_MODEL)
    lq1, lk1, lq2, lk2 = (t[0].reshape(1, HEAD_DIM) for t in (lambda_q1, lambda_k1, lambda_q2, lambda_k2))
    subln = subln_b[0].reshape(1, V_DIM)
    head_of_lane = jnp.arange(D_INNER) // SSM_HEAD_DIM
    hexp = (jnp.arange(LANES)[:, None] == head_of_lane[None, :]).astype(F32)
    gexp = (jnp.arange(N_GROUPS)[:, None] == (head_of_lane // (SSM_HEADS // N_GROUPS))[None, :]).astype(F32)

    xp = x_prompt.reshape(mp, D_MODEL)
    h1, conv_p, ssm_p = _ssd_prompt(xp, bp, g_a, wz, wx, wdt, convw, convb, dtb, alog, dskip, gnorm, wout_a)
    tables_p = _rope_tables(jnp.arange(lp, dtype=F32))
    kt_p, v_p, kb_p, vt_p, qt_p, gate_p = _kvq_flash(h1, g_kv, g_b, wkv, wb, *tables_p, b=bp, l=lp)
    k_p = jnp.transpose(kt_p.reshape(bp, ATT_HEADS, 2, HEAD_DIM, lp), (0, 4, 1, 2, 3))

    n_pool = cache_k.shape[0]
    cache_kt = jnp.transpose(cache_k, (0, 2, 3, 4, 1)).reshape(n_pool, K_DIM, PAGE_SIZE)
    cache_vr = cache_v.reshape(n_pool, PAGE_SIZE * ATT_HEADS, V_DIM)
    sconv = jnp.transpose(state_conv[0], (1, 0, 2))
    xs_in = x_sample.reshape(bs, D_MODEL)
    z_s, xbc_s, dt_s = _inproj_a(xs_in, g_a, wz, wx, wdt, tm=bs)
    conv_s, xdt, dec, bm, cm, ydiag, xs_act = _ssd_step_pre(xbc_s, dt_s, sconv, convw, convb, dtb, alog, hexp, gexp)
    yoff, ssm_s = _ssd_step_state(xdt, dec, bm, cm, state_ssm[0])
    h1_s = _ssd_step_out(ydiag, yoff, dec, xs_act, z_s, xs_in, dskip, gnorm, wout_a)
    tables_s = _rope_tables(jnp.full((bs,), float(PAST_LEN), F32))
    k_s, v_s, q_s, gate_s = _kvq(h1_s, g_kv, g_b, wkv, wb, *tables_s, tm=bs)

    o_p, o_s = _attention(qt_p, kb_p.reshape(bp, lp, K_DIM), vt_p, page_table, q_s, k_s, v_s, cache_kt, cache_vr,
                          lq1, lk1, lq2, lk2, subln, tq=512)
    y_p = _attn_out(o_p.reshape(mp, VAL_WIDTH), gate_p, h1, wout_b, g_f, tm=256)
    y_s = _attn_out(o_s, gate_s, h1_s, wout_b, g_f, tm=bs)

    return (y_p.reshape(bp, lp, D_MODEL), y_s.reshape(bs, 1, D_MODEL),
            k_p, v_p.reshape(bp, lp, ATT_HEADS, V_DIM),
            conv_p[None], ssm_p[None],
            k_s.reshape(bs, 1, ATT_HEADS, 2, HEAD_DIM), v_s.reshape(bs, 1, ATT_HEADS, V_DIM),
            jnp.transpose(conv_s, (1, 0, 2))[None], ssm_s[None])
```

```python
import functools
import math

import jax
import jax.numpy as jnp
from jax import lax
from jax.experimental import pallas as pl
from jax.experimental.pallas import tpu as pltpu

F32 = jnp.float32
BF16 = jnp.bfloat16
HIGHEST = lax.Precision.HIGHEST

D_MODEL = 1024
PAST_LEN = 8192
PAGE_SIZE = 128
N_PAGES = PAST_LEN // PAGE_SIZE
D_INNER = 2048
SSM_HEAD_DIM = 64
SSM_HEADS = 32
N_GROUPS = 4
D_STATE = 128
D_CONV = 4
CONV_DIM = D_INNER + 2 * N_GROUPS * D_STATE
CHUNK = 128
ATT_HEADS = 8
HEAD_DIM = 64
V_DIM = 128
K_DIM = 1024
VAL_WIDTH = 1024
ROT_DIM = 16
ROPE_THETA = 500000.0
EPS = 1e-6
NEG_INF = -1e30
LAMBDA_INIT = 0.8 - 0.6 * math.exp(-0.3 * 1)

FLASH_TK = 256
VT_ROWS = V_DIM + 16
LOG2_E = math.log2(math.e)
Q_SCALE_LOG2 = HEAD_DIM ** -0.5 * LOG2_E

LANES = 128
VMEM_LIMIT = 56 * 1024 * 1024

NT_DIMS = (((1,), (1,)), ((), ()))


def _params(*sem):
    return pltpu.CompilerParams(dimension_semantics=sem, vmem_limit_bytes=VMEM_LIMIT)


def _silu(x):
    return x * (1.0 / (1.0 + jnp.exp(-x)))


def _softplus(x):
    return jnp.maximum(x, 0.0) + jnp.log1p(jnp.exp(-jnp.abs(x)))


def _rms(x):
    return x * lax.rsqrt(jnp.mean(x * x, axis=-1, keepdims=True) + EPS)


def _dot(a, b):
    return jnp.dot(a, b, preferred_element_type=F32)


def _full(shape):
    return pl.BlockSpec(shape, lambda *_: (0,) * len(shape))


def _inproj_a_kernel(x_ref, g_ref, wz_ref, wx_ref, wdt_ref, z_ref, xbc_ref, dt_ref):
    xn = (_rms(x_ref[...]) * g_ref[...]).astype(BF16)
    z_ref[...] = _dot(xn, wz_ref[...])
    xbc_ref[...] = _dot(xn, wx_ref[...])
    dt_ref[...] = _dot(xn, wdt_ref[...])


def _inproj_a(x, g, wz, wx, wdt, tm):
    m = x.shape[0]
    row = lambda n: pl.BlockSpec((tm, n), lambda i: (i, 0))
    return pl.pallas_call(
        _inproj_a_kernel,
        grid=(m // tm,),
        in_specs=[row(D_MODEL), _full((1, D_MODEL)), _full(wz.shape), _full(wx.shape), _full(wdt.shape)],
        out_specs=[row(D_INNER), row(CONV_DIM), row(LANES)],
        out_shape=[jax.ShapeDtypeStruct((m, D_INNER), F32),
                   jax.ShapeDtypeStruct((m, CONV_DIM), F32),
                   jax.ShapeDtypeStruct((m, LANES), F32)],
        compiler_params=_params("arbitrary"),
        name="inproj_a",
    )(x, g, wz, wx, wdt)


def _gate_norm_outproj(y, gate, gnorm, wout, xres):
    yg = y * gate
    gw = D_INNER // N_GROUPS
    parts = [_rms(yg[:, g * gw:(g + 1) * gw]) for g in range(N_GROUPS)]
    yn = (jnp.concatenate(parts, axis=1) * gnorm).astype(BF16)
    return xres + _dot(yn, wout)


def _project_chunk(x, g_ref, wz_ref, wx_ref, wdt_ref, proj):
    z_ref, xbc_ref, dt_ref = proj
    xn = (_rms(x) * g_ref[...]).astype(BF16)
    z_ref[...] = _dot(xn, wz_ref[...])
    xbc_ref[...] = _dot(xn, wx_ref[...])
    dt_ref[...] = _dot(xn, wdt_ref[...])


PROJ_COLS = 512


def _scan_chunk(proj, xres, convw_ref, convb_ref, dtb_ref, alog_ref, dskip_ref, gnorm_ref, wout_ref,
                xpad_ref, st_ref, work, x_ahead, g_ref, wz_ref, wx_ref, wdt_ref, proj_ahead):
    z_ref, xbc_ref, dt_ref = proj
    za_ref, xbca_ref, dta_ref = proj_ahead
    act_ref, gate_ref, y_ref = work
    q = CHUNK
    xn = (_rms(x_ahead) * g_ref[...]).astype(BF16)
    sub8 = lax.broadcasted_iota(jnp.int32, (q // 8, 8, PROJ_COLS), 1)

    for blk in range(CONV_DIM // PROJ_COLS):
        cols = slice(blk * PROJ_COLS, (blk + 1) * PROJ_COLS)
        xbca_ref[:, cols] = _dot(xn, wx_ref[:, cols])
        xbc = xbc_ref[:, cols]
        xpad_ref[8:8 + q, cols] = xbc
        conv = (convb_ref[:, cols] + xbc * convw_ref[3:4, cols]).reshape(q // 8, 8, PROJ_COLS)
        groups = xpad_ref[:, cols].reshape(q // 8 + 1, 8, PROJ_COLS)
        for k in range(1, D_CONV):
            rot = pltpu.roll(groups, k, 1)
            back_k = jnp.where(sub8 < k, rot[:q // 8], rot[1:])
            conv = conv + back_k * convw_ref[D_CONV - 1 - k:D_CONV - k, cols]
        act_ref[:, cols] = _silu(conv).reshape(q, PROJ_COLS)
        xpad_ref[0:8, cols] = xpad_ref[q:q + 8, cols]
    for blk in range(D_INNER // PROJ_COLS):
        cols = slice(blk * PROJ_COLS, (blk + 1) * PROJ_COLS)
        za_ref[:, cols] = _dot(xn, wz_ref[:, cols])
        gate_ref[:, cols] = _silu(z_ref[:, cols])
    dta_ref[...] = _dot(xn, wdt_ref[...])

    dt = _softplus(dt_ref[...] + dtb_ref[...])
    a = -jnp.exp(alog_ref[...])
    row_i = lax.broadcasted_iota(jnp.int32, (q, q), 0)
    col_i = lax.broadcasted_iota(jnp.int32, (q, q), 1)
    causal = row_i >= col_i
    a_cs = jnp.dot(causal.astype(F32), dt * a, precision=HIGHEST, preferred_element_type=F32)
    a_cs_t = a_cs.T
    dt_t = dt.T
    w_t = dt_t * jnp.exp(a_cs_t[:, q - 1:q] - a_cs_t)
    e_cs = jnp.exp(a_cs)
    first_half = col_i < SSM_HEAD_DIM

    for g in range(N_GROUPS):
        bm = act_ref[:, D_INNER + g * D_STATE:D_INNER + (g + 1) * D_STATE]
        cm = act_ref[:, D_INNER + (N_GROUPS + g) * D_STATE:D_INNER + (N_GROUPS + g + 1) * D_STATE]
        bm_b = bm.astype(BF16)
        cm_b = cm.astype(BF16)
        cb = lax.dot_general(cm_b, bm_b, NT_DIMS, preferred_element_type=F32)
        bm_t = bm.T
        for jj in range(SSM_HEADS // N_GROUPS // 2):
            j = g * (SSM_HEADS // N_GROUPS // 2) + jj
            xs = act_ref[:, j * LANES:(j + 1) * LANES]
            xs_b = xs.astype(BF16)
            zero = jnp.zeros_like(xs_b)
            xbd = jnp.concatenate([jnp.where(first_half, xs_b, zero),
                                   jnp.where(first_half, zero, xs_b)], axis=0)
            m_parts, w_parts = [], []
            for h in (2 * j, 2 * j + 1):
                seg = a_cs[:, h:h + 1] - a_cs_t[h:h + 1, :]
                decay = jnp.exp(jnp.where(causal, seg, -jnp.inf))
                m_parts.append((cb * decay * dt_t[h:h + 1, :]).astype(BF16))
                w_parts.append((bm_t * w_t[h:h + 1, :]).astype(BF16))
            both = _dot(jnp.concatenate([jnp.concatenate(m_parts, axis=1), jnp.concatenate(w_parts, axis=1)], axis=0),
                        xbd)
            y_diag, upd_t = both[:q], both[q:]
            st = st_ref[j]
            e_sel = jnp.where(first_half, e_cs[:, 2 * j:2 * j + 1], e_cs[:, 2 * j + 1:2 * j + 2])
            y_off = _dot(cm_b, st.astype(BF16)) * e_sel
            st_ref[j] = st * e_sel[q - 1:q, :] + upd_t
            y_ref[:, j * LANES:(j + 1) * LANES] = y_diag + y_off + dskip_ref[:, j * LANES:(j + 1) * LANES] * xs

    return _gate_norm_outproj(y_ref[...], gate_ref[...], gnorm_ref[...], wout_ref[...], xres)


def _ssd_prompt_kernel(x2_ref, xnext_ref, g_ref, wz_ref, wx_ref, wdt_ref, convw_ref, convb_ref, dtb_ref, alog_ref,
                       dskip_ref, gnorm_ref, wout_ref,
                       h_ref, conv_out_ref, ssm_out_ref,
                       z0_ref, xbc0_ref, dt0_ref, z1_ref, xbc1_ref, dt1_ref,
                       act0_ref, gate0_ref, y0_ref, act1_ref, gate1_ref, y1_ref, xpad_ref, st_ref,
                       *, steps_per_seq):
    i = pl.program_id(0)
    q = CHUNK
    proj0, proj1 = (z0_ref, xbc0_ref, dt0_ref), (z1_ref, xbc1_ref, dt1_ref)
    weights = (g_ref, wz_ref, wx_ref, wdt_ref)
    consts = (convw_ref, convb_ref, dtb_ref, alog_ref, dskip_ref, gnorm_ref, wout_ref)

    @pl.when(i % steps_per_seq == 0)
    def _():
        xpad_ref[0:8, :] = jnp.zeros((8, CONV_DIM), F32)
        st_ref[...] = jnp.zeros_like(st_ref)

    @pl.when(i == 0)
    def _():
        _project_chunk(x2_ref[0:q, :], *weights, proj0)

    h_ref[0:q, :] = _scan_chunk(proj0, x2_ref[0:q, :], *consts, xpad_ref, st_ref, (act0_ref, gate0_ref, y0_ref),
                                x2_ref[q:2 * q, :], *weights, proj1)
    h_ref[q:2 * q, :] = _scan_chunk(proj1, x2_ref[q:2 * q, :], *consts, xpad_ref, st_ref,
                                    (act1_ref, gate1_ref, y1_ref), xnext_ref[...], *weights, proj0)

    @pl.when(i % steps_per_seq == steps_per_seq - 1)
    def _():
        conv_out_ref[0] = xpad_ref[5:8, :]
        for j in range(SSM_HEADS // 2):
            ssm_out_ref[0, 2 * j:2 * j + 2] = st_ref[j].T.reshape(2, SSM_HEAD_DIM, D_STATE)


def _ssd_prompt(x, b, g, wz, wx, wdt, convw, convb, dtb, alog, dskip, gnorm, wout):
    m = x.shape[0]
    steps = m // (2 * CHUNK)
    steps_per_seq = steps // b
    n_chunks = m // CHUNK
    vm = lambda *shape: pltpu.VMEM(shape, F32)
    per_seq = lambda *blk: pl.BlockSpec((1,) + blk, lambda i: (i // steps_per_seq,) + (0,) * len(blk))
    return pl.pallas_call(
        functools.partial(_ssd_prompt_kernel, steps_per_seq=steps_per_seq),
        grid=(steps,),
        in_specs=[pl.BlockSpec((2 * CHUNK, D_MODEL), lambda i: (i, 0)),
                  pl.BlockSpec((CHUNK, D_MODEL), lambda i: (jnp.minimum(2 * i + 2, n_chunks - 1), 0)),
                  _full(g.shape), _full(wz.shape), _full(wx.shape), _full(wdt.shape),
                  _full(convw.shape), _full(convb.shape), _full(dtb.shape), _full(alog.shape),
                  _full(dskip.shape), _full(gnorm.shape), _full(wout.shape)],
        out_specs=[pl.BlockSpec((2 * CHUNK, D_MODEL), lambda i: (i, 0)),
                   per_seq(D_CONV - 1, CONV_DIM),
                   per_seq(SSM_HEADS, SSM_HEAD_DIM, D_STATE)],
        out_shape=[jax.ShapeDtypeStruct((m, D_MODEL), F32),
                   jax.ShapeDtypeStruct((b, D_CONV - 1, CONV_DIM), F32),
                   jax.ShapeDtypeStruct((b, SSM_HEADS, SSM_HEAD_DIM, D_STATE), F32)],
        scratch_shapes=[vm(CHUNK, D_INNER), vm(CHUNK, CONV_DIM), vm(CHUNK, LANES),
                        vm(CHUNK, D_INNER), vm(CHUNK, CONV_DIM), vm(CHUNK, LANES),
                        vm(CHUNK, CONV_DIM), vm(CHUNK, D_INNER), vm(CHUNK, D_INNER),
                        vm(CHUNK, CONV_DIM), vm(CHUNK, D_INNER), vm(CHUNK, D_INNER),
                        vm(CHUNK + 8, CONV_DIM),
                        vm(SSM_HEADS // 2, D_STATE, LANES)],
        compiler_params=_params("arbitrary"),
        name="ssd_prompt",
    )(x, x, g, wz, wx, wdt, convw, convb, dtb, alog, dskip, gnorm, wout)


def _ssd_step_pre_kernel(xbc_ref, dt_ref, sconv_ref, convw_ref, convb_ref, dtb_ref, alog_ref,
                         hexp_ref, gexp_ref,
                         conv_out_ref, xdt_ref, dec_ref, bm_ref, cm_ref, ydiag_ref, xs_ref):
    xbc = xbc_ref[...]
    conv = convb_ref[...] + xbc * convw_ref[3:4, :]
    for k in range(D_CONV - 1):
        conv = conv + sconv_ref[k] * convw_ref[k:k + 1, :]
    conv_out_ref[0] = sconv_ref[1]
    conv_out_ref[1] = sconv_ref[2]
    conv_out_ref[2] = xbc
    act = _silu(conv)
    xs = act[:, :D_INNER]
    bm = act[:, D_INNER:D_INNER + N_GROUPS * D_STATE]
    cm = act[:, D_INNER + N_GROUPS * D_STATE:]
    dt = _softplus(dt_ref[...] + dtb_ref[...])
    dec = jnp.exp(dt * -jnp.exp(alog_ref[...]))
    hexp = hexp_ref[...]
    dt_x = jnp.dot(dt, hexp, precision=HIGHEST, preferred_element_type=F32)
    dec_x = jnp.dot(dec, hexp, precision=HIGHEST, preferred_element_type=F32)
    prod = bm * cm
    cb = jnp.concatenate([jnp.sum(prod[:, g * D_STATE:(g + 1) * D_STATE], axis=1, keepdims=True)
                          for g in range(N_GROUPS)], axis=1)
    cb_x = jnp.dot(cb, gexp_ref[...], precision=HIGHEST, preferred_element_type=F32)
    xdt = xs * dt_x
    xdt_ref[...] = xdt
    dec_ref[...] = dec_x
    bm_ref[...] = bm
    cm_ref[...] = cm
    ydiag_ref[...] = cb_x * xdt
    xs_ref[...] = xs


def _ssd_step_pre(xbc, dt, sconv, convw, convb, dtb, alog, hexp, gexp):
    m = xbc.shape[0]
    gs = N_GROUPS * D_STATE
    return pl.pallas_call(
        _ssd_step_pre_kernel,
        out_shape=[jax.ShapeDtypeStruct((D_CONV - 1, m, CONV_DIM), F32),
                   jax.ShapeDtypeStruct((m, D_INNER), F32),
                   jax.ShapeDtypeStruct((m, D_INNER), F32),
                   jax.ShapeDtypeStruct((m, gs), F32),
                   jax.ShapeDtypeStruct((m, gs), F32),
                   jax.ShapeDtypeStruct((m, D_INNER), F32),
                   jax.ShapeDtypeStruct((m, D_INNER), F32)],
        compiler_params=pltpu.CompilerParams(vmem_limit_bytes=VMEM_LIMIT),
        name="ssd_step_pre",
    )(xbc, dt, sconv, convw, convb, dtb, alog, hexp, gexp)


def _ssd_step_state_kernel(xdt_ref, dec_ref, bm_ref, cm_ref, st_ref, yoff_ref, st_out_ref):
    gw = D_INNER // N_GROUPS
    hg = SSM_HEADS // N_GROUPS
    for g in range(N_GROUPS):
        xdt_rows = jnp.broadcast_to(xdt_ref[0, :, g * gw:(g + 1) * gw], (D_STATE, gw)).T
        dec_rows = jnp.broadcast_to(dec_ref[0, :, g * gw:(g + 1) * gw], (D_STATE, gw)).T
        bm = bm_ref[0, :, g * D_STATE:(g + 1) * D_STATE]
        cm8 = jnp.broadcast_to(cm_ref[0, :, g * D_STATE:(g + 1) * D_STATE], (8, D_STATE)).astype(BF16)
        st = st_ref[0, g * hg:(g + 1) * hg].reshape(gw, D_STATE)
        y8 = lax.dot_general(cm8, st.astype(BF16), NT_DIMS, preferred_element_type=F32)
        yoff_ref[0, :, g * gw:(g + 1) * gw] = y8[0:1, :]
        st_out_ref[0, g * hg:(g + 1) * hg] = (st * dec_rows + xdt_rows * bm).reshape(hg, SSM_HEAD_DIM, D_STATE)


def _ssd_step_state(xdt, dec, bm, cm, st):
    m = xdt.shape[0]
    gs = N_GROUPS * D_STATE
    row = lambda n: pl.BlockSpec((1, 1, n), lambda i: (i, 0, 0))
    st_spec = pl.BlockSpec((1, SSM_HEADS, SSM_HEAD_DIM, D_STATE), lambda i: (i, 0, 0, 0))
    yoff, st_out = pl.pallas_call(
        _ssd_step_state_kernel,
        grid=(m,),
        in_specs=[row(D_INNER), row(D_INNER), row(gs), row(gs), st_spec],
        out_specs=[row(D_INNER), st_spec],
        out_shape=[jax.ShapeDtypeStruct((m, 1, D_INNER), F32),
                   jax.ShapeDtypeStruct(st.shape, F32)],
        compiler_params=_params("arbitrary"),
        name="ssd_step_state",
    )(xdt.reshape(m, 1, D_INNER), dec.reshape(m, 1, D_INNER), bm.reshape(m, 1, gs), cm.reshape(m, 1, gs), st)
    return yoff.reshape(m, D_INNER), st_out


def _ssd_step_out_kernel(ydiag_ref, yoff_ref, dec_ref, xs_ref, z_ref, x_ref, dskip_ref, gnorm_ref, wout_ref,
                         h_ref):
    y = ydiag_ref[...] + yoff_ref[...] * dec_ref[...] + dskip_ref[...] * xs_ref[...]
    h_ref[...] = _gate_norm_outproj(y, _silu(z_ref[...]), gnorm_ref[...], wout_ref[...], x_ref[...])


def _ssd_step_out(ydiag, yoff, dec, xs, z, x, dskip, gnorm, wout):
    return pl.pallas_call(
        _ssd_step_out_kernel,
        out_shape=jax.ShapeDtypeStruct(x.shape, F32),
        compiler_params=pltpu.CompilerParams(vmem_limit_bytes=VMEM_LIMIT),
        name="ssd_step_out",
    )(ydiag, yoff, dec, xs, z, x, dskip, gnorm, wout)


def _rope(t, cos, sin_up, sin_dn):
    cols = []
    for c in range(t.shape[1] // LANES):
        tc = t[:, c * LANES:(c + 1) * LANES]
        cols.append(tc * cos + pltpu.roll(tc, ROT_DIM // 2, 1) * sin_up
                    + pltpu.roll(tc, LANES - ROT_DIM // 2, 1) * sin_dn)
    return jnp.concatenate(cols, axis=1)


def _kvq_kernel(h_ref, gkv_ref, gb_ref, wkv_ref, wb_ref, cos_ref, sup_ref, sdn_ref, *out_refs, for_flash):
    xh = _rms(h_ref[...])
    kv = _dot((xh * gkv_ref[...]).astype(BF16), wkv_ref[...])
    qg = _dot((xh * gb_ref[...]).astype(BF16), wb_ref[...])
    cos, sup, sdn = cos_ref[...], sup_ref[...], sdn_ref[...]
    k = _rope(kv[:, :K_DIM], cos, sup, sdn)
    v = kv[:, K_DIM:]
    q = _rope(qg[:, :K_DIM], cos, sup, sdn)
    if for_flash:
        kt_ref, v_ref, kb_ref, vt_ref, qt_ref, gate_ref = out_refs
        kt_ref[0] = k.T
        kb_ref[...] = k.astype(BF16)
        vt = v.T.astype(BF16)
        for j in range(vt_ref.shape[1]):
            for h in range(ATT_HEADS):
                vt_ref[0, j, h, :V_DIM, :] = vt[h * V_DIM:(h + 1) * V_DIM, j * FLASH_TK:(j + 1) * FLASH_TK]
                vt_ref[0, j, h, V_DIM:, :] = jnp.ones((VT_ROWS - V_DIM, FLASH_TK), BF16)
        qt_ref[0] = (q * Q_SCALE_LOG2).T.astype(BF16)
    else:
        k_ref, v_ref, q_ref, gate_ref = out_refs
        k_ref[...] = k
        q_ref[...] = q
    v_ref[...] = v
    gate_ref[...] = qg[:, K_DIM:]


def _kvq_in_specs(tm, wkv, wb, table_blocks):
    table = pl.BlockSpec((tm, LANES), lambda i: (i % table_blocks, 0))
    return [pl.BlockSpec((tm, D_MODEL), lambda i: (i, 0)), _full((1, D_MODEL)), _full((1, D_MODEL)),
            _full(wkv.shape), _full(wb.shape), table, table, table]


def _kvq(h, gkv, gb, wkv, wb, cos, sup, sdn, tm):
    m = h.shape[0]
    row = pl.BlockSpec((tm, D_MODEL), lambda i: (i, 0))
    out = jax.ShapeDtypeStruct((m, D_MODEL), F32)
    return pl.pallas_call(
        functools.partial(_kvq_kernel, for_flash=False),
        grid=(m // tm,),
        in_specs=_kvq_in_specs(tm, wkv, wb, cos.shape[0] // tm),
        out_specs=[row] * 4,
        out_shape=[out] * 4,
        compiler_params=_params("arbitrary"),
        name="kvq_proj",
    )(h, gkv, gb, wkv, wb, cos, sup, sdn)


def _kvq_flash(h, gkv, gb, wkv, wb, cos, sup, sdn, b, l):
    kpt = 2
    tm = kpt * FLASH_TK
    nb = l // tm
    row = pl.BlockSpec((tm, D_MODEL), lambda i: (i, 0))
    feat = pl.BlockSpec((1, D_MODEL, tm), lambda i: (i // nb, 0, i % nb))
    return pl.pallas_call(
        functools.partial(_kvq_kernel, for_flash=True),
        grid=(b * nb,),
        in_specs=_kvq_in_specs(tm, wkv, wb, nb),
        out_specs=[feat, row, row,
                   pl.BlockSpec((1, kpt, ATT_HEADS, VT_ROWS, FLASH_TK), lambda i: (i // nb, i % nb, 0, 0, 0)),
                   feat, row],
        out_shape=[jax.ShapeDtypeStruct((b, K_DIM, l), F32),
                   jax.ShapeDtypeStruct((b * l, VAL_WIDTH), F32),
                   jax.ShapeDtypeStruct((b * l, K_DIM), BF16),
                   jax.ShapeDtypeStruct((b, l // FLASH_TK, ATT_HEADS, VT_ROWS, FLASH_TK), BF16),
                   jax.ShapeDtypeStruct((b, K_DIM, l), BF16),
                   jax.ShapeDtypeStruct((b * l, VAL_WIDTH), F32)],
        compiler_params=_params("arbitrary"),
        name="kvq_proj_flash",
    )(h, gkv, gb, wkv, wb, cos, sup, sdn)


def _rope_tables(pos):
    inv_freq = ROPE_THETA ** (-jnp.arange(0, ROT_DIM, 2, dtype=F32) / ROT_DIM)
    ang = pos[:, None] * inv_freq[None, :]
    cos, sin = jnp.cos(ang), jnp.sin(ang)
    half = ROT_DIM // 2
    rows = pos.shape[0]
    pad = jnp.zeros((rows, HEAD_DIM - ROT_DIM), F32)
    zero = jnp.zeros((rows, half), F32)
    cos64 = jnp.concatenate([cos, cos, pad + 1.0], axis=1)
    sup64 = jnp.concatenate([zero, sin, pad], axis=1)
    sdn64 = jnp.concatenate([-sin, zero, pad], axis=1)
    return tuple(jnp.tile(t, (1, LANES // HEAD_DIM)) for t in (cos64, sup64, sdn64))


def _lambda(lq1_ref, lk1_ref, lq2_ref, lk2_ref):
    s1 = jnp.sum(lq1_ref[...] * lk1_ref[...], axis=1, keepdims=True)
    s2 = jnp.sum(lq2_ref[...] * lk2_ref[...], axis=1, keepdims=True)
    return jnp.exp(s1) - jnp.exp(s2) + LAMBDA_INIT


def _sub_norm(o, subln):
    return _rms(o) * subln * (1.0 - LAMBDA_INIT)


PAGES_PER_STEP = 8
N_MAPS = 2 * ATT_HEADS


def _attention_kernel(pt_ref, qt_ref, k_ref, vt_ref, lq1_ref, lk1_ref, lq2_ref, lk2_ref, subln_col_ref,
                      subln_row_ref, spread_ref, qs_ref, kn_ref, vn_ref, *rest, tq, tk, groups):
    kt_refs = rest[:PAGES_PER_STEP]
    v_refs = rest[PAGES_PER_STEP:2 * PAGES_PER_STEP]
    (o_ref, os_ref, acc_ref, sa_ref, sb_ref, qrow_ref, ms_ref, ls_ref, accs_ref, ss_ref,
     ps_ref) = rest[2 * PAGES_PER_STEP:]
    qi = pl.program_id(2)
    step = (pl.program_id(0) * pl.num_programs(1) + pl.program_id(1)) * pl.num_programs(2) + qi
    group = step % groups
    n = 2 * tq
    half = PAGES_PER_STEP // 2

    @pl.when(group == 0)
    def _():
        map_i = lax.broadcasted_iota(jnp.int32, (N_MAPS, K_DIM), 0)
        seg = 2 * (map_i % ATT_HEADS) + map_i // ATT_HEADS
        lane_seg = lax.broadcasted_iota(jnp.int32, (N_MAPS, K_DIM), 1) // HEAD_DIM
        qs = qs_ref[0] * (HEAD_DIM ** -0.5)
        qrow = jnp.where(lane_seg == seg, jnp.broadcast_to(qs, (N_MAPS, K_DIM)), 0.0)
        qrow_ref[...] = qrow.astype(BF16)
        ms_ref[...] = jnp.sum(qrow * kn_ref[0], axis=1, keepdims=True)
        ls_ref[...] = jnp.ones_like(ls_ref)
        accs_ref[...] = jnp.concatenate([vn_ref[0], vn_ref[0]], axis=0)

    qt = qt_ref[0]
    feat = lax.broadcasted_iota(jnp.int32, (LANES, tq), 0)
    zero = jnp.zeros_like(qt)
    qqt = jnp.concatenate([jnp.where(feat < HEAD_DIM, qt, zero), jnp.where(feat < HEAD_DIM, zero, qt)], axis=1)
    acc_ref[...] = jnp.zeros_like(acc_ref)

    def logits(kb, lanes=slice(None)):
        start = pl.multiple_of(kb * tk, tk)
        return _dot(k_ref[0, pl.ds(start, tk), :], qqt[:, lanes])

    def softmax_pv(st, m, kb, lanes=slice(None)):
        m_new = jnp.maximum(m, jnp.max(st, axis=0, keepdims=True))
        alpha = jnp.exp2(m - m_new)
        p = jnp.exp2(st - m_new).astype(BF16)
        acc_ref[:, lanes] = acc_ref[:, lanes] * alpha + _dot(vt_ref[0, kb, 0], p)
        return m_new

    qrow = qrow_ref[...]
    for i in range(0, PAGES_PER_STEP, 2):
        ss_ref[:, i * PAGE_SIZE:(i + 2) * PAGE_SIZE] = _dot(
            qrow, jnp.concatenate([kt_refs[i][0].astype(BF16), kt_refs[i + 1][0].astype(BF16)], axis=1))
        if i == half - 2:
            sa_ref[...] = logits(0)

    def pair(i, m):
        kb = 2 * i
        sb_ref[...] = logits(kb + 1)
        m = softmax_pv(sa_ref[...], m, kb)
        sa_ref[...] = logits(kb + 2)
        return softmax_pv(sb_ref[...], m, kb + 1)

    m = lax.fori_loop(0, qi, pair, jnp.full((1, n), NEG_INF, F32))

    s = ss_ref[...]
    ms_old = ms_ref[...]
    ms_new = jnp.maximum(ms_old, jnp.max(s, axis=1, keepdims=True))
    alpha_s = jnp.exp(ms_old - ms_new)
    ps = jnp.exp(s - ms_new)
    ls_ref[...] = alpha_s * ls_ref[...] + jnp.sum(ps, axis=1, keepdims=True)
    ms_ref[...] = ms_new
    p_rows = jnp.concatenate([ps[:, i * PAGE_SIZE:(i + 1) * PAGE_SIZE] for i in range(PAGES_PER_STEP)], axis=0)
    p_wide = _dot(p_rows.astype(BF16), spread_ref[...])
    own_head = (lax.broadcasted_iota(jnp.int32, p_wide.shape, 0) % ATT_HEADS
                == lax.broadcasted_iota(jnp.int32, p_wide.shape, 1) % ATT_HEADS)
    ps_ref[...] = jnp.where(own_head, p_wide, 0.0).astype(BF16)

    def page_values(pages):
        pv = jnp.zeros((N_MAPS, V_DIM), F32)
        for i in pages[::2]:
            two = _dot(ps_ref[i * N_MAPS:(i + 2) * N_MAPS, :],
                       jnp.concatenate([v_refs[i][0].astype(BF16), v_refs[i + 1][0].astype(BF16)], axis=1))
            pv = pv + two[:N_MAPS, :V_DIM] + two[N_MAPS:, V_DIM:]
        return pv

    kb = 2 * qi
    late = (slice(tk, tq), slice(tq + tk, n))
    tri = lax.broadcasted_iota(jnp.int32, (tk, tk), 0) <= lax.broadcasted_iota(jnp.int32, (tk, tk), 1)
    late_logits = [jnp.where(tri, logits(kb + 1, lanes), NEG_INF) for lanes in late]
    pv = page_values(range(half))
    key_i = lax.broadcasted_iota(jnp.int32, (tk, n), 0)
    qry_i = lax.broadcasted_iota(jnp.int32, (tk, n), 1) % tq
    m = softmax_pv(jnp.where(key_i <= qry_i, sa_ref[...], NEG_INF), m, kb)
    pv = pv + page_values(range(half, PAGES_PER_STEP))
    accs_ref[...] = alpha_s * accs_ref[...] + pv
    for lanes, st in zip(late, late_logits):
        softmax_pv(st, m[:, lanes], kb + 1, lanes)

    lam = _lambda(lq1_ref, lk1_ref, lq2_ref, lk2_ref)
    on = acc_ref[:V_DIM, :] * (1.0 / acc_ref[V_DIM:V_DIM + 1, :])
    ot = on[:, :tq] - lam * on[:, tq:]
    ot = ot * lax.rsqrt(jnp.mean(ot * ot, axis=0, keepdims=True) + EPS) * subln_col_ref[...] * (1.0 - LAMBDA_INIT)
    o_ref[0] = ot.T.astype(o_ref.dtype)

    @pl.when(group == groups - 1)
    def _():
        ons = accs_ref[...] / ls_ref[...]
        os_ref[0] = _sub_norm(ons[:ATT_HEADS] - lam * ons[ATT_HEADS:], subln_row_ref[...])


def _attention(qt, kb, vt, page_table, qs, kn, vn, cache_kt, cache_v, lq1, lk1, lq2, lk2, subln, tq):
    b, _, l = qt.shape
    m = qs.shape[0]
    tk = FLASH_TK
    nq = l // tq
    groups = N_PAGES // PAGES_PER_STEP
    assert tq == 2 * tk
    assert b * ATT_HEADS * nq == m * groups
    spread = (jnp.arange(PAGE_SIZE)[:, None] == jnp.arange(PAGE_SIZE * ATT_HEADS)[None, :] // ATT_HEADS).astype(BF16)

    def step(i, h, t):
        return (i * ATT_HEADS + h) * nq + t

    small = pl.BlockSpec((1, HEAD_DIM), lambda i, h, t, pt: (0, 0))
    row = pl.BlockSpec((1, 1, D_MODEL), lambda i, h, t, pt: (step(i, h, t) // groups, 0, 0))
    heads = pl.BlockSpec((1, ATT_HEADS, V_DIM), lambda i, h, t, pt: (step(i, h, t) // groups, 0, 0))

    def page(r):
        def index(i, h, t, pt):
            s = step(i, h, t)
            return (pt[(s // groups) * N_PAGES + (s % groups) * PAGES_PER_STEP + r], 0, 0)
        return pl.BlockSpec((1, K_DIM, PAGE_SIZE), index)

    pages = [page(r) for r in range(PAGES_PER_STEP)]
    grid_spec = pltpu.PrefetchScalarGridSpec(
        num_scalar_prefetch=1,
        grid=(b, ATT_HEADS, nq),
        in_specs=[pl.BlockSpec((1, LANES, tq), lambda i, h, t, pt: (i, h, t)),
                  pl.BlockSpec((1, l, LANES), lambda i, h, t, pt: (i, 0, h)),
                  pl.BlockSpec((1, l // tk, 1, VT_ROWS, tk), lambda i, h, t, pt: (i, 0, h, 0, 0)),
                  small, small, small, small,
                  pl.BlockSpec((V_DIM, 1), lambda i, h, t, pt: (0, 0)),
                  pl.BlockSpec((1, V_DIM), lambda i, h, t, pt: (0, 0)),
                  pl.BlockSpec(spread.shape, lambda i, h, t, pt: (0, 0)),
                  row, row, heads] + pages + pages,
        out_specs=[pl.BlockSpec((1, tq, LANES), lambda i, h, t, pt: (i, t, h)), heads],
        scratch_shapes=[pltpu.VMEM((VT_ROWS, 2 * tq), F32),
                        pltpu.VMEM((tk, 2 * tq), F32),
                        pltpu.VMEM((tk, 2 * tq), F32),
                        pltpu.VMEM((N_MAPS, K_DIM), BF16),
                        pltpu.VMEM((N_MAPS, 1), F32),
                        pltpu.VMEM((N_MAPS, 1), F32),
                        pltpu.VMEM((N_MAPS, V_DIM), F32),
                        pltpu.VMEM((N_MAPS, PAGES_PER_STEP * PAGE_SIZE), F32),
                        pltpu.VMEM((PAGES_PER_STEP * N_MAPS, PAGE_SIZE * ATT_HEADS), BF16)],
    )
    o, o_s = pl.pallas_call(
        functools.partial(_attention_kernel, tq=tq, tk=tk, groups=groups),
        grid_spec=grid_spec,
        out_shape=[jax.ShapeDtypeStruct((b, l, VAL_WIDTH), BF16),
                   jax.ShapeDtypeStruct((m, ATT_HEADS, V_DIM), F32)],
        compiler_params=_params("arbitrary", "arbitrary", "arbitrary"),
        name="attention",
    )(page_table.reshape(-1), qt, kb, vt, lq1, lk1, lq2, lk2, subln.reshape(V_DIM, 1), subln, spread,
      qs.reshape(m, 1, K_DIM), kn.reshape(m, 1, K_DIM), vn.reshape(m, ATT_HEADS, V_DIM),
      *([cache_kt] * PAGES_PER_STEP), *([cache_v] * PAGES_PER_STEP))
    return o, o_s.reshape(m, VAL_WIDTH)


def _attn_out_kernel(o_ref, gate_ref, h_ref, wout_ref, gf_ref, y_ref):
    og = (o_ref[...] * _silu(gate_ref[...])).astype(BF16)
    y_ref[...] = _rms(h_ref[...] + _dot(og, wout_ref[...])) * gf_ref[...]


def _attn_out(o, gate, h, wout, gf, tm):
    m = o.shape[0]
    row = pl.BlockSpec((tm, D_MODEL), lambda i: (i, 0))
    return pl.pallas_call(
        _attn_out_kernel,
        grid=(m // tm,),
        in_specs=[row, row, row, _full(wout.shape), _full((1, D_MODEL))],
        out_specs=row,
        out_shape=jax.ShapeDtypeStruct((m, D_MODEL), F32),
        compiler_params=_params("arbitrary"),
        name="attn_out",
    )(o, gate, h, wout, gf)


def _pad_lanes(v):
    return jnp.pad(v.reshape(1, -1), ((0, 0), (0, LANES - v.shape[-1])))


def kernel(x_prompt, x_sample, cache_k, cache_v, page_table, state_conv, state_ssm, norm_a, w_in_a, conv_w, conv_b, dt_bias, a_log, d_skip, gnorm_a, w_out_a, norm_kv, w_kv, norm_b, w_in_b, lambda_q1, lambda_k1, lambda_q2, lambda_k2, subln_b, w_out_b, norm_f):
    bp, lp, _ = x_prompt.shape
    bs = x_sample.shape[0]
    mp = bp * lp

    w_a = w_in_a[0].astype(BF16)
    wz, wx = w_a[:, :D_INNER], w_a[:, D_INNER:D_INNER + CONV_DIM]
    wdt = jnp.pad(w_a[:, D_INNER + CONV_DIM:], ((0, 0), (0, LANES - SSM_HEADS)))
    wout_a = w_out_a[0].astype(BF16)
    wkv = w_kv.astype(BF16)
    wb = w_in_b[0].astype(BF16)
    wout_b = w_out_b[0].astype(BF16)
    g_a = norm_a[0].reshape(1, D_MODEL)
    convw, convb = conv_w[0], conv_b[0].reshape(1, CONV_DIM)
    dtb, alog = _pad_lanes(dt_bias[0]), _pad_lanes(a_log[0])
    dskip = jnp.repeat(d_skip[0], SSM_HEAD_DIM).reshape(1, D_INNER)
    gnorm = gnorm_a[0].reshape(1, D_INNER)
    g_kv, g_b, g_f = norm_kv.reshape(1, D_MODEL), norm_b[0].reshape(1, D_MODEL), norm_f.reshape(1, D_MODEL)
    lq1, lk1, lq2, lk2 = (t[0].reshape(1, HEAD_DIM) for t in (lambda_q1, lambda_k1, lambda_q2, lambda_k2))
    subln = subln_b[0].reshape(1, V_DIM)
    head_of_lane = jnp.arange(D_INNER) // SSM_HEAD_DIM
    hexp = (jnp.arange(LANES)[:, None] == head_of_lane[None, :]).astype(F32)
    gexp = (jnp.arange(N_GROUPS)[:, None] == (head_of_lane // (SSM_HEADS // N_GROUPS))[None, :]).astype(F32)

    xp = x_prompt.reshape(mp, D_MODEL)
    h1, conv_p, ssm_p = _ssd_prompt(xp, bp, g_a, wz, wx, wdt, convw, convb, dtb, alog, dskip, gnorm, wout_a)
    tables_p = _rope_tables(jnp.arange(lp, dtype=F32))
    kt_p, v_p, kb_p, vt_p, qt_p, gate_p = _kvq_flash(h1, g_kv, g_b, wkv, wb, *tables_p, b=bp, l=lp)
    k_p = jnp.transpose(kt_p.reshape(bp, ATT_HEADS, 2, HEAD_DIM, lp), (0, 4, 1, 2, 3))

    n_pool = cache_k.shape[0]
    cache_kt = jnp.transpose(cache_k, (0, 2, 3, 4, 1)).reshape(n_pool, K_DIM, PAGE_SIZE)
    cache_vr = cache_v.reshape(n_pool, PAGE_SIZE * ATT_HEADS, V_DIM)
    sconv = jnp.transpose(state_conv[0], (1, 0, 2))
    xs_in = x_sample.reshape(bs, D_MODEL)
    z_s, xbc_s, dt_s = _inproj_a(xs_in, g_a, wz, wx, wdt, tm=bs)
    conv_s, xdt, dec, bm, cm, ydiag, xs_act = _ssd_step_pre(xbc_s, dt_s, sconv, convw, convb, dtb, alog, hexp, gexp)
    yoff, ssm_s = _ssd_step_state(xdt, dec, bm, cm, state_ssm[0])
    h1_s = _ssd_step_out(ydiag, yoff, dec, xs_act, z_s, xs_in, dskip, gnorm, wout_a)
    tables_s = _rope_tables(jnp.full((bs,), float(PAST_LEN), F32))
    k_s, v_s, q_s, gate_s = _kvq(h1_s, g_kv, g_b, wkv, wb, *tables_s, tm=bs)

    o_p, o_s = _attention(qt_p, kb_p.reshape(bp, lp, K_DIM), vt_p, page_table, q_s, k_s, v_s, cache_kt, cache_vr,
                          lq1, lk1, lq2, lk2, subln, tq=512)
    y_p = _attn_out(o_p.reshape(mp, VAL_WIDTH), gate_p, h1, wout_b, g_f, tm=256)
    y_s = _attn_out(o_s, gate_s, h1_s, wout_b, g_f, tm=bs)

    return (y_p.reshape(bp, lp, D_MODEL), y_s.reshape(bs, 1, D_MODEL),
            k_p, v_p.reshape(bp, lp, ATT_HEADS, V_DIM),
            conv_p[None], ssm_p[None],
            k_s.reshape(bs, 1, ATT_HEADS, 2, HEAD_DIM), v_s.reshape(bs, 1, ATT_HEADS, V_DIM),
            jnp.transpose(conv_s, (1, 0, 2))[None], ssm_s[None])
```

```python
import functools
import math

import jax
import jax.numpy as jnp
from jax import lax
from jax.experimental import pallas as pl
from jax.experimental.pallas import tpu as pltpu

F32 = jnp.float32
BF16 = jnp.bfloat16
HIGHEST = lax.Precision.HIGHEST

D_MODEL = 1024
PAST_LEN = 8192
PAGE_SIZE = 128
N_PAGES = PAST_LEN // PAGE_SIZE
D_INNER = 2048
SSM_HEAD_DIM = 64
SSM_HEADS = 32
N_GROUPS = 4
D_STATE = 128
D_CONV = 4
CONV_DIM = D_INNER + 2 * N_GROUPS * D_STATE
CHUNK = 128
ATT_HEADS = 8
HEAD_DIM = 64
V_DIM = 128
K_DIM = 1024
VAL_WIDTH = 1024
ROT_DIM = 16
ROPE_THETA = 500000.0
EPS = 1e-6
NEG_INF = -1e30
LAMBDA_INIT = 0.8 - 0.6 * math.exp(-0.3 * 1)

FLASH_TK = 256
VT_ROWS = V_DIM + 16
LOG2_E = math.log2(math.e)
Q_SCALE_LOG2 = HEAD_DIM ** -0.5 * LOG2_E

LANES = 128
VMEM_LIMIT = 56 * 1024 * 1024

NT_DIMS = (((1,), (1,)), ((), ()))


def _params(*sem):
    return pltpu.CompilerParams(dimension_semantics=sem, vmem_limit_bytes=VMEM_LIMIT)


def _silu(x):
    return x * (1.0 / (1.0 + jnp.exp(-x)))


def _softplus(x):
    return jnp.maximum(x, 0.0) + jnp.log1p(jnp.exp(-jnp.abs(x)))


def _rms(x):
    return x * lax.rsqrt(jnp.mean(x * x, axis=-1, keepdims=True) + EPS)


def _dot(a, b):
    return jnp.dot(a, b, preferred_element_type=F32)


def _full(shape):
    return pl.BlockSpec(shape, lambda *_: (0,) * len(shape))


def _inproj_a_kernel(x_ref, g_ref, wz_ref, wx_ref, wdt_ref, z_ref, xbc_ref, dt_ref):
    xn = (_rms(x_ref[...]) * g_ref[...]).astype(BF16)
    z_ref[...] = _dot(xn, wz_ref[...])
    xbc_ref[...] = _dot(xn, wx_ref[...])
    dt_ref[...] = _dot(xn, wdt_ref[...])


def _inproj_a(x, g, wz, wx, wdt, tm):
    m = x.shape[0]
    row = lambda n: pl.BlockSpec((tm, n), lambda i: (i, 0))
    return pl.pallas_call(
        _inproj_a_kernel,
        grid=(m // tm,),
        in_specs=[row(D_MODEL), _full((1, D_MODEL)), _full(wz.shape), _full(wx.shape), _full(wdt.shape)],
        out_specs=[row(D_INNER), row(CONV_DIM), row(LANES)],
        out_shape=[jax.ShapeDtypeStruct((m, D_INNER), F32),
                   jax.ShapeDtypeStruct((m, CONV_DIM), F32),
                   jax.ShapeDtypeStruct((m, LANES), F32)],
        compiler_params=_params("arbitrary"),
        name="inproj_a",
    )(x, g, wz, wx, wdt)


def _gate_norm_outproj(y, gate, gnorm, wout, xres):
    yg = y * gate
    gw = D_INNER // N_GROUPS
    parts = [_rms(yg[:, g * gw:(g + 1) * gw]) for g in range(N_GROUPS)]
    yn = (jnp.concatenate(parts, axis=1) * gnorm).astype(BF16)
    return xres + _dot(yn, wout)


def _project_chunk(x, g_ref, wz_ref, wx_ref, wdt_ref, proj):
    z_ref, xbc_ref, dt_ref = proj
    xn = (_rms(x) * g_ref[...]).astype(BF16)
    z_ref[...] = _dot(xn, wz_ref[...])
    xbc_ref[...] = _dot(xn, wx_ref[...])
    dt_ref[...] = _dot(xn, wdt_ref[...])


PROJ_COLS = 512


def _scan_chunk(proj, xres, convw_ref, convb_ref, dtb_ref, alog_ref, dskip_ref, gnorm_ref, wout_ref,
                xpad_ref, st_ref, work, x_ahead, g_ref, wz_ref, wx_ref, wdt_ref, proj_ahead):
    z_ref, xbc_ref, dt_ref = proj
    za_ref, xbca_ref, dta_ref = proj_ahead
    act_ref, gate_ref, y_ref = work
    q = CHUNK
    xn = (_rms(x_ahead) * g_ref[...]).astype(BF16)
    sub8 = lax.broadcasted_iota(jnp.int32, (q // 8, 8, PROJ_COLS), 1)

    for blk in range(CONV_DIM // PROJ_COLS):
        cols = slice(blk * PROJ_COLS, (blk + 1) * PROJ_COLS)
        xbca_ref[:, cols] = _dot(xn, wx_ref[:, cols])
        xbc = xbc_ref[:, cols]
        xpad_ref[8:8 + q, cols] = xbc
        conv = (convb_ref[:, cols] + xbc * convw_ref[3:4, cols]).reshape(q // 8, 8, PROJ_COLS)
        groups = xpad_ref[:, cols].reshape(q // 8 + 1, 8, PROJ_COLS)
        for k in range(1, D_CONV):
            rot = pltpu.roll(groups, k, 1)
            back_k = jnp.where(sub8 < k, rot[:q // 8], rot[1:])
            conv = conv + back_k * convw_ref[D_CONV - 1 - k:D_CONV - k, cols]
        act_ref[:, cols] = _silu(conv).reshape(q, PROJ_COLS)
        xpad_ref[0:8, cols] = xpad_ref[q:q + 8, cols]
    for blk in range(D_INNER // PROJ_COLS):
        cols = slice(blk * PROJ_COLS, (blk + 1) * PROJ_COLS)
        za_ref[:, cols] = _dot(xn, wz_ref[:, cols])
        gate_ref[:, cols] = _silu(z_ref[:, cols])
    dta_ref[...] = _dot(xn, wdt_ref[...])

    dt = _softplus(dt_ref[...] + dtb_ref[...])
    a = -jnp.exp(alog_ref[...])
    row_i = lax.broadcasted_iota(jnp.int32, (q, q), 0)
    col_i = lax.broadcasted_iota(jnp.int32, (q, q), 1)
    causal = row_i >= col_i
    a_cs = jnp.dot(causal.astype(F32), dt * a, precision=HIGHEST, preferred_element_type=F32)
    a_cs_t = a_cs.T
    dt_t = dt.T
    w_t = dt_t * jnp.exp(a_cs_t[:, q - 1:q] - a_cs_t)
    e_cs = jnp.exp(a_cs)
    first_half = col_i < SSM_HEAD_DIM

    for g in range(N_GROUPS):
        bm = act_ref[:, D_INNER + g * D_STATE:D_INNER + (g + 1) * D_STATE]
        cm = act_ref[:, D_INNER + (N_GROUPS + g) * D_STATE:D_INNER + (N_GROUPS + g + 1) * D_STATE]
        bm_b = bm.astype(BF16)
        cm_b = cm.astype(BF16)
        cb = lax.dot_general(cm_b, bm_b, NT_DIMS, preferred_element_type=F32)
        bm_t = bm.T
        for jj in range(SSM_HEADS // N_GROUPS // 2):
            j = g * (SSM_HEADS // N_GROUPS // 2) + jj
            xs = act_ref[:, j * LANES:(j + 1) * LANES]
            xs_b = xs.astype(BF16)
            zero = jnp.zeros_like(xs_b)
            xbd = jnp.concatenate([jnp.where(first_half, xs_b, zero),
                                   jnp.where(first_half, zero, xs_b)], axis=0)
            m_parts, w_parts = [], []
            for h in (2 * j, 2 * j + 1):
                seg = a_cs[:, h:h + 1] - a_cs_t[h:h + 1, :]
                decay = jnp.exp(jnp.where(causal, seg, -jnp.inf))
                m_parts.append((cb * decay * dt_t[h:h + 1, :]).astype(BF16))
                w_parts.append((bm_t * w_t[h:h + 1, :]).astype(BF16))
            y_diag = _dot(jnp.concatenate(m_parts, axis=1), xbd)
            upd_t = _dot(jnp.concatenate(w_parts, axis=1), xbd)
            st = st_ref[j]
            e_sel = jnp.where(first_half, e_cs[:, 2 * j:2 * j + 1], e_cs[:, 2 * j + 1:2 * j + 2])
            y_off = _dot(cm_b, st.astype(BF16)) * e_sel
            st_ref[j] = st * e_sel[q - 1:q, :] + upd_t
            y_ref[:, j * LANES:(j + 1) * LANES] = y_diag + y_off + dskip_ref[:, j * LANES:(j + 1) * LANES] * xs

    return _gate_norm_outproj(y_ref[...], gate_ref[...], gnorm_ref[...], wout_ref[...], xres)


def _ssd_prompt_kernel(x2_ref, xnext_ref, g_ref, wz_ref, wx_ref, wdt_ref, convw_ref, convb_ref, dtb_ref, alog_ref,
                       dskip_ref, gnorm_ref, wout_ref,
                       h_ref, conv_out_ref, ssm_out_ref,
                       z0_ref, xbc0_ref, dt0_ref, z1_ref, xbc1_ref, dt1_ref,
                       act0_ref, gate0_ref, y0_ref, act1_ref, gate1_ref, y1_ref, xpad_ref, st_ref,
                       *, steps_per_seq):
    i = pl.program_id(0)
    q = CHUNK
    proj0, proj1 = (z0_ref, xbc0_ref, dt0_ref), (z1_ref, xbc1_ref, dt1_ref)
    weights = (g_ref, wz_ref, wx_ref, wdt_ref)
    consts = (convw_ref, convb_ref, dtb_ref, alog_ref, dskip_ref, gnorm_ref, wout_ref)

    @pl.when(i % steps_per_seq == 0)
    def _():
        xpad_ref[0:8, :] = jnp.zeros((8, CONV_DIM), F32)
        st_ref[...] = jnp.zeros_like(st_ref)

    @pl.when(i == 0)
    def _():
        _project_chunk(x2_ref[0:q, :], *weights, proj0)

    h_ref[0:q, :] = _scan_chunk(proj0, x2_ref[0:q, :], *consts, xpad_ref, st_ref, (act0_ref, gate0_ref, y0_ref),
                                x2_ref[q:2 * q, :], *weights, proj1)
    h_ref[q:2 * q, :] = _scan_chunk(proj1, x2_ref[q:2 * q, :], *consts, xpad_ref, st_ref,
                                    (act1_ref, gate1_ref, y1_ref), xnext_ref[...], *weights, proj0)

    @pl.when(i % steps_per_seq == steps_per_seq - 1)
    def _():
        conv_out_ref[0] = xpad_ref[5:8, :]
        for j in range(SSM_HEADS // 2):
            ssm_out_ref[0, 2 * j:2 * j + 2] = st_ref[j].T.reshape(2, SSM_HEAD_DIM, D_STATE)


def _ssd_prompt(x, b, g, wz, wx, wdt, convw, convb, dtb, alog, dskip, gnorm, wout):
    m = x.shape[0]
    steps = m // (2 * CHUNK)
    steps_per_seq = steps // b
    n_chunks = m // CHUNK
    vm = lambda *shape: pltpu.VMEM(shape, F32)
    per_seq = lambda *blk: pl.BlockSpec((1,) + blk, lambda i: (i // steps_per_seq,) + (0,) * len(blk))
    return pl.pallas_call(
        functools.partial(_ssd_prompt_kernel, steps_per_seq=steps_per_seq),
        grid=(steps,),
        in_specs=[pl.BlockSpec((2 * CHUNK, D_MODEL), lambda i: (i, 0)),
                  pl.BlockSpec((CHUNK, D_MODEL), lambda i: (jnp.minimum(2 * i + 2, n_chunks - 1), 0)),
                  _full(g.shape), _full(wz.shape), _full(wx.shape), _full(wdt.shape),
                  _full(convw.shape), _full(convb.shape), _full(dtb.shape), _full(alog.shape),
                  _full(dskip.shape), _full(gnorm.shape), _full(wout.shape)],
        out_specs=[pl.BlockSpec((2 * CHUNK, D_MODEL), lambda i: (i, 0)),
                   per_seq(D_CONV - 1, CONV_DIM),
                   per_seq(SSM_HEADS, SSM_HEAD_DIM, D_STATE)],
        out_shape=[jax.ShapeDtypeStruct((m, D_MODEL), F32),
                   jax.ShapeDtypeStruct((b, D_CONV - 1, CONV_DIM), F32),
                   jax.ShapeDtypeStruct((b, SSM_HEADS, SSM_HEAD_DIM, D_STATE), F32)],
        scratch_shapes=[vm(CHUNK, D_INNER), vm(CHUNK, CONV_DIM), vm(CHUNK, LANES),
                        vm(CHUNK, D_INNER), vm(CHUNK, CONV_DIM), vm(CHUNK, LANES),
                        vm(CHUNK, CONV_DIM), vm(CHUNK, D_INNER), vm(CHUNK, D_INNER),
                        vm(CHUNK, CONV_DIM), vm(CHUNK, D_INNER), vm(CHUNK, D_INNER),
                        vm(CHUNK + 8, CONV_DIM),
                        vm(SSM_HEADS // 2, D_STATE, LANES)],
        compiler_params=_params("arbitrary"),
        name="ssd_prompt",
    )(x, x, g, wz, wx, wdt, convw, convb, dtb, alog, dskip, gnorm, wout)


def _ssd_step_pre_kernel(xbc_ref, dt_ref, sconv_ref, convw_ref, convb_ref, dtb_ref, alog_ref,
                         hexp_ref, gexp_ref,
                         conv_out_ref, xdt_ref, dec_ref, bm_ref, cm_ref, ydiag_ref, xs_ref):
    xbc = xbc_ref[...]
    conv = convb_ref[...] + xbc * convw_ref[3:4, :]
    for k in range(D_CONV - 1):
        conv = conv + sconv_ref[k] * convw_ref[k:k + 1, :]
    conv_out_ref[0] = sconv_ref[1]
    conv_out_ref[1] = sconv_ref[2]
    conv_out_ref[2] = xbc
    act = _silu(conv)
    xs = act[:, :D_INNER]
    bm = act[:, D_INNER:D_INNER + N_GROUPS * D_STATE]
    cm = act[:, D_INNER + N_GROUPS * D_STATE:]
    dt = _softplus(dt_ref[...] + dtb_ref[...])
    dec = jnp.exp(dt * -jnp.exp(alog_ref[...]))
    hexp = hexp_ref[...]
    dt_x = jnp.dot(dt, hexp, precision=HIGHEST, preferred_element_type=F32)
    dec_x = jnp.dot(dec, hexp, precision=HIGHEST, preferred_element_type=F32)
    prod = bm * cm
    cb = jnp.concatenate([jnp.sum(prod[:, g * D_STATE:(g + 1) * D_STATE], axis=1, keepdims=True)
                          for g in range(N_GROUPS)], axis=1)
    cb_x = jnp.dot(cb, gexp_ref[...], precision=HIGHEST, preferred_element_type=F32)
    xdt = xs * dt_x
    xdt_ref[...] = xdt
    dec_ref[...] = dec_x
    bm_ref[...] = bm
    cm_ref[...] = cm
    ydiag_ref[...] = cb_x * xdt
    xs_ref[...] = xs


def _ssd_step_pre(xbc, dt, sconv, convw, convb, dtb, alog, hexp, gexp):
    m = xbc.shape[0]
    gs = N_GROUPS * D_STATE
    return pl.pallas_call(
        _ssd_step_pre_kernel,
        out_shape=[jax.ShapeDtypeStruct((D_CONV - 1, m, CONV_DIM), F32),
                   jax.ShapeDtypeStruct((m, D_INNER), F32),
                   jax.ShapeDtypeStruct((m, D_INNER), F32),
                   jax.ShapeDtypeStruct((m, gs), F32),
                   jax.ShapeDtypeStruct((m, gs), F32),
                   jax.ShapeDtypeStruct((m, D_INNER), F32),
                   jax.ShapeDtypeStruct((m, D_INNER), F32)],
        compiler_params=pltpu.CompilerParams(vmem_limit_bytes=VMEM_LIMIT),
        name="ssd_step_pre",
    )(xbc, dt, sconv, convw, convb, dtb, alog, hexp, gexp)


def _ssd_step_state_kernel(xdt_ref, dec_ref, bm_ref, cm_ref, st_ref, yoff_ref, st_out_ref):
    gw = D_INNER // N_GROUPS
    hg = SSM_HEADS // N_GROUPS
    for g in range(N_GROUPS):
        xdt_rows = jnp.broadcast_to(xdt_ref[0, :, g * gw:(g + 1) * gw], (D_STATE, gw)).T
        dec_rows = jnp.broadcast_to(dec_ref[0, :, g * gw:(g + 1) * gw], (D_STATE, gw)).T
        bm = bm_ref[0, :, g * D_STATE:(g + 1) * D_STATE]
        cm8 = jnp.broadcast_to(cm_ref[0, :, g * D_STATE:(g + 1) * D_STATE], (8, D_STATE)).astype(BF16)
        st = st_ref[0, g * hg:(g + 1) * hg].reshape(gw, D_STATE)
        y8 = lax.dot_general(cm8, st.astype(BF16), NT_DIMS, preferred_element_type=F32)
        yoff_ref[0, :, g * gw:(g + 1) * gw] = y8[0:1, :]
        st_out_ref[0, g * hg:(g + 1) * hg] = (st * dec_rows + xdt_rows * bm).reshape(hg, SSM_HEAD_DIM, D_STATE)


def _ssd_step_state(xdt, dec, bm, cm, st):
    m = xdt.shape[0]
    gs = N_GROUPS * D_STATE
    row = lambda n: pl.BlockSpec((1, 1, n), lambda i: (i, 0, 0))
    st_spec = pl.BlockSpec((1, SSM_HEADS, SSM_HEAD_DIM, D_STATE), lambda i: (i, 0, 0, 0))
    yoff, st_out = pl.pallas_call(
        _ssd_step_state_kernel,
        grid=(m,),
        in_specs=[row(D_INNER), row(D_INNER), row(gs), row(gs), st_spec],
        out_specs=[row(D_INNER), st_spec],
        out_shape=[jax.ShapeDtypeStruct((m, 1, D_INNER), F32),
                   jax.ShapeDtypeStruct(st.shape, F32)],
        compiler_params=_params("arbitrary"),
        name="ssd_step_state",
    )(xdt.reshape(m, 1, D_INNER), dec.reshape(m, 1, D_INNER), bm.reshape(m, 1, gs), cm.reshape(m, 1, gs), st)
    return yoff.reshape(m, D_INNER), st_out


def _ssd_step_out_kernel(ydiag_ref, yoff_ref, dec_ref, xs_ref, z_ref, x_ref, dskip_ref, gnorm_ref, wout_ref,
                         h_ref):
    y = ydiag_ref[...] + yoff_ref[...] * dec_ref[...] + dskip_ref[...] * xs_ref[...]
    h_ref[...] = _gate_norm_outproj(y, _silu(z_ref[...]), gnorm_ref[...], wout_ref[...], x_ref[...])


def _ssd_step_out(ydiag, yoff, dec, xs, z, x, dskip, gnorm, wout):
    return pl.pallas_call(
        _ssd_step_out_kernel,
        out_shape=jax.ShapeDtypeStruct(x.shape, F32),
        compiler_params=pltpu.CompilerParams(vmem_limit_bytes=VMEM_LIMIT),
        name="ssd_step_out",
    )(ydiag, yoff, dec, xs, z, x, dskip, gnorm, wout)


def _rope(t, cos, sin_up, sin_dn):
    cols = []
    for c in range(t.shape[1] // LANES):
        tc = t[:, c * LANES:(c + 1) * LANES]
        cols.append(tc * cos + pltpu.roll(tc, ROT_DIM // 2, 1) * sin_up
                    + pltpu.roll(tc, LANES - ROT_DIM // 2, 1) * sin_dn)
    return jnp.concatenate(cols, axis=1)


def _kvq_kernel(h_ref, gkv_ref, gb_ref, wkv_ref, wb_ref, cos_ref, sup_ref, sdn_ref, *out_refs, for_flash):
    xh = _rms(h_ref[...])
    kv = _dot((xh * gkv_ref[...]).astype(BF16), wkv_ref[...])
    qg = _dot((xh * gb_ref[...]).astype(BF16), wb_ref[...])
    cos, sup, sdn = cos_ref[...], sup_ref[...], sdn_ref[...]
    k = _rope(kv[:, :K_DIM], cos, sup, sdn)
    v = kv[:, K_DIM:]
    q = _rope(qg[:, :K_DIM], cos, sup, sdn)
    if for_flash:
        kt_ref, v_ref, kb_ref, vt_ref, qt_ref, gate_ref = out_refs
        kt_ref[0] = k.T
        kb_ref[...] = k.astype(BF16)
        vt = v.T.astype(BF16)
        for j in range(vt_ref.shape[1]):
            for h in range(ATT_HEADS):
                vt_ref[0, j, h, :V_DIM, :] = vt[h * V_DIM:(h + 1) * V_DIM, j * FLASH_TK:(j + 1) * FLASH_TK]
                vt_ref[0, j, h, V_DIM:, :] = jnp.ones((VT_ROWS - V_DIM, FLASH_TK), BF16)
        qt_ref[0] = (q * Q_SCALE_LOG2).T.astype(BF16)
    else:
        k_ref, v_ref, q_ref, gate_ref = out_refs
        k_ref[...] = k
        q_ref[...] = q
    v_ref[...] = v
    gate_ref[...] = qg[:, K_DIM:]


def _kvq_in_specs(tm, wkv, wb, table_blocks):
    table = pl.BlockSpec((tm, LANES), lambda i: (i % table_blocks, 0))
    return [pl.BlockSpec((tm, D_MODEL), lambda i: (i, 0)), _full((1, D_MODEL)), _full((1, D_MODEL)),
            _full(wkv.shape), _full(wb.shape), table, table, table]


def _kvq(h, gkv, gb, wkv, wb, cos, sup, sdn, tm):
    m = h.shape[0]
    row = pl.BlockSpec((tm, D_MODEL), lambda i: (i, 0))
    out = jax.ShapeDtypeStruct((m, D_MODEL), F32)
    return pl.pallas_call(
        functools.partial(_kvq_kernel, for_flash=False),
        grid=(m // tm,),
        in_specs=_kvq_in_specs(tm, wkv, wb, cos.shape[0] // tm),
        out_specs=[row] * 4,
        out_shape=[out] * 4,
        compiler_params=_params("arbitrary"),
        name="kvq_proj",
    )(h, gkv, gb, wkv, wb, cos, sup, sdn)


def _kvq_flash(h, gkv, gb, wkv, wb, cos, sup, sdn, b, l):
    kpt = 2
    tm = kpt * FLASH_TK
    nb = l // tm
    row = pl.BlockSpec((tm, D_MODEL), lambda i: (i, 0))
    feat = pl.BlockSpec((1, D_MODEL, tm), lambda i: (i // nb, 0, i % nb))
    return pl.pallas_call(
        functools.partial(_kvq_kernel, for_flash=True),
        grid=(b * nb,),
        in_specs=_kvq_in_specs(tm, wkv, wb, nb),
        out_specs=[feat, row, row,
                   pl.BlockSpec((1, kpt, ATT_HEADS, VT_ROWS, FLASH_TK), lambda i: (i // nb, i % nb, 0, 0, 0)),
                   feat, row],
        out_shape=[jax.ShapeDtypeStruct((b, K_DIM, l), F32),
                   jax.ShapeDtypeStruct((b * l, VAL_WIDTH), F32),
                   jax.ShapeDtypeStruct((b * l, K_DIM), BF16),
                   jax.ShapeDtypeStruct((b, l // FLASH_TK, ATT_HEADS, VT_ROWS, FLASH_TK), BF16),
                   jax.ShapeDtypeStruct((b, K_DIM, l), BF16),
                   jax.ShapeDtypeStruct((b * l, VAL_WIDTH), F32)],
        compiler_params=_params("arbitrary"),
        name="kvq_proj_flash",
    )(h, gkv, gb, wkv, wb, cos, sup, sdn)


def _rope_tables(pos):
    inv_freq = ROPE_THETA ** (-jnp.arange(0, ROT_DIM, 2, dtype=F32) / ROT_DIM)
    ang = pos[:, None] * inv_freq[None, :]
    cos, sin = jnp.cos(ang), jnp.sin(ang)
    half = ROT_DIM // 2
    rows = pos.shape[0]
    pad = jnp.zeros((rows, HEAD_DIM - ROT_DIM), F32)
    zero = jnp.zeros((rows, half), F32)
    cos64 = jnp.concatenate([cos, cos, pad + 1.0], axis=1)
    sup64 = jnp.concatenate([zero, sin, pad], axis=1)
    sdn64 = jnp.concatenate([-sin, zero, pad], axis=1)
    return tuple(jnp.tile(t, (1, LANES // HEAD_DIM)) for t in (cos64, sup64, sdn64))


def _lambda(lq1_ref, lk1_ref, lq2_ref, lk2_ref):
    s1 = jnp.sum(lq1_ref[...] * lk1_ref[...], axis=1, keepdims=True)
    s2 = jnp.sum(lq2_ref[...] * lk2_ref[...], axis=1, keepdims=True)
    return jnp.exp(s1) - jnp.exp(s2) + LAMBDA_INIT


def _sub_norm(o, subln):
    return _rms(o) * subln * (1.0 - LAMBDA_INIT)


PAGES_PER_STEP = 16
N_MAPS = 2 * ATT_HEADS


class _FlashTile:
    def __init__(self, qi, qt_ref, k_ref, vt_ref, acc_ref, sa_ref, sb_ref, tq, tk):
        self.qi, self.k_ref, self.vt_ref = qi, k_ref, vt_ref
        self.acc_ref, self.sa_ref, self.sb_ref, self.tq, self.tk = acc_ref, sa_ref, sb_ref, tq, tk
        qt = qt_ref[0]
        feat = lax.broadcasted_iota(jnp.int32, (LANES, tq), 0)
        zero = jnp.zeros_like(qt)
        self.qqt = jnp.concatenate([jnp.where(feat < HEAD_DIM, qt, zero), jnp.where(feat < HEAD_DIM, zero, qt)],
                                   axis=1)

    def logits(self, kb, lanes=slice(None)):
        start = pl.multiple_of(kb * self.tk, self.tk)
        return _dot(self.k_ref[0, pl.ds(start, self.tk), :], self.qqt[:, lanes])

    def softmax_pv(self, st, m, kb, lanes=slice(None)):
        m_new = jnp.maximum(m, jnp.max(st, axis=0, keepdims=True))
        alpha = jnp.exp2(m - m_new)
        p = jnp.exp2(st - m_new).astype(BF16)
        self.acc_ref[:, lanes] = self.acc_ref[:, lanes] * alpha + _dot(self.vt_ref[0, kb, 0], p)
        return m_new

    def start(self):
        self.acc_ref[...] = jnp.zeros_like(self.acc_ref)
        self.sa_ref[...] = self.logits(0)

    def below_diagonal(self):
        def pair(i, m):
            kb = 2 * i
            self.sb_ref[...] = self.logits(kb + 1)
            m = self.softmax_pv(self.sa_ref[...], m, kb)
            self.sa_ref[...] = self.logits(kb + 2)
            return self.softmax_pv(self.sb_ref[...], m, kb + 1)

        return lax.fori_loop(0, self.qi, pair, jnp.full((1, 2 * self.tq), NEG_INF, F32))

    def late_lanes(self):
        return (slice(self.tk, self.tq), slice(self.tq + self.tk, 2 * self.tq))

    def late_logits(self):
        tk = self.tk
        tri = lax.broadcasted_iota(jnp.int32, (tk, tk), 0) <= lax.broadcasted_iota(jnp.int32, (tk, tk), 1)
        return [jnp.where(tri, self.logits(2 * self.qi + 1, lanes), NEG_INF) for lanes in self.late_lanes()]

    def diagonal_first(self, m):
        n = 2 * self.tq
        key_i = lax.broadcasted_iota(jnp.int32, (self.tk, n), 0)
        qry_i = lax.broadcasted_iota(jnp.int32, (self.tk, n), 1) % self.tq
        return self.softmax_pv(jnp.where(key_i <= qry_i, self.sa_ref[...], NEG_INF), m, 2 * self.qi)

    def diagonal_second(self, m, late_logits):
        for lanes, st in zip(self.late_lanes(), late_logits):
            self.softmax_pv(st, m[:, lanes], 2 * self.qi + 1, lanes)

    def finish(self, lam, subln_col):
        acc_ref, tq = self.acc_ref, self.tq
        on = acc_ref[:V_DIM, :] * (1.0 / acc_ref[V_DIM:V_DIM + 1, :])
        ot = on[:, :tq] - lam * on[:, tq:]
        ot = ot * lax.rsqrt(jnp.mean(ot * ot, axis=0, keepdims=True) + EPS) * subln_col * (1.0 - LAMBDA_INIT)
        return ot.T


def _attention_kernel(pt_ref, qta_ref, qtb_ref, k_ref, vt_ref, lq1_ref, lk1_ref, lq2_ref, lk2_ref, subln_col_ref,
                      subln_row_ref, spread_ref, qs_ref, kn_ref, vn_ref, *rest, tq, tk, groups):
    kt_refs = rest[:PAGES_PER_STEP]
    v_refs = rest[PAGES_PER_STEP:2 * PAGES_PER_STEP]
    (o_ref, os_ref, acca_ref, saa_ref, sba_ref, accb_ref, sab_ref, sbb_ref,
     qrow_ref, ms_ref, ls_ref, accs_ref, ss_ref, ps_ref) = rest[2 * PAGES_PER_STEP:]
    t = pl.program_id(2)
    nt = pl.num_programs(2)
    step = (pl.program_id(0) * pl.num_programs(1) + pl.program_id(1)) * nt + t
    group = step % groups
    quarter = PAGES_PER_STEP // 4

    @pl.when(group == 0)
    def _():
        map_i = lax.broadcasted_iota(jnp.int32, (N_MAPS, K_DIM), 0)
        seg = 2 * (map_i % ATT_HEADS) + map_i // ATT_HEADS
        lane_seg = lax.broadcasted_iota(jnp.int32, (N_MAPS, K_DIM), 1) // HEAD_DIM
        qs = qs_ref[0] * (HEAD_DIM ** -0.5)
        qrow = jnp.where(lane_seg == seg, jnp.broadcast_to(qs, (N_MAPS, K_DIM)), 0.0)
        qrow_ref[...] = qrow.astype(BF16)
        ms_ref[...] = jnp.sum(qrow * kn_ref[0], axis=1, keepdims=True)
        ls_ref[...] = jnp.ones_like(ls_ref)
        accs_ref[...] = jnp.concatenate([vn_ref[0], vn_ref[0]], axis=0)

    tile_a = _FlashTile(t, qta_ref, k_ref, vt_ref, acca_ref, saa_ref, sba_ref, tq, tk)
    tile_b = _FlashTile(2 * nt - 1 - t, qtb_ref, k_ref, vt_ref, accb_ref, sab_ref, sbb_ref, tq, tk)

    qrow = qrow_ref[...]

    def page_scores(pages):
        for i in pages[::2]:
            ss_ref[:, i * PAGE_SIZE:(i + 2) * PAGE_SIZE] = _dot(
                qrow, jnp.concatenate([kt_refs[i][0].astype(BF16), kt_refs[i + 1][0].astype(BF16)], axis=1))

    page_scores(range(2 * quarter))
    tile_a.start()
    page_scores(range(2 * quarter, PAGES_PER_STEP))
    tile_b.start()
    m_a = tile_a.below_diagonal()
    m_b = tile_b.below_diagonal()

    s = ss_ref[...]
    ms_old = ms_ref[...]
    ms_new = jnp.maximum(ms_old, jnp.max(s, axis=1, keepdims=True))
    alpha_s = jnp.exp(ms_old - ms_new)
    ps = jnp.exp(s - ms_new)
    ls_ref[...] = alpha_s * ls_ref[...] + jnp.sum(ps, axis=1, keepdims=True)
    ms_ref[...] = ms_new
    p_rows = jnp.concatenate([ps[:, i * PAGE_SIZE:(i + 1) * PAGE_SIZE] for i in range(PAGES_PER_STEP)], axis=0)
    p_wide = _dot(p_rows.astype(BF16), spread_ref[...])
    own_head = (lax.broadcasted_iota(jnp.int32, p_wide.shape, 0) % ATT_HEADS
                == lax.broadcasted_iota(jnp.int32, p_wide.shape, 1) % ATT_HEADS)
    ps_ref[...] = jnp.where(own_head, p_wide, 0.0).astype(BF16)

    def page_values(pages):
        pv = jnp.zeros((N_MAPS, V_DIM), F32)
        for i in pages[::2]:
            two = _dot(ps_ref[i * N_MAPS:(i + 2) * N_MAPS, :],
                       jnp.concatenate([v_refs[i][0].astype(BF16), v_refs[i + 1][0].astype(BF16)], axis=1))
            pv = pv + two[:N_MAPS, :V_DIM] + two[N_MAPS:, V_DIM:]
        return pv

    late_a = tile_a.late_logits()
    late_b = tile_b.late_logits()
    pv = page_values(range(quarter))
    m_a = tile_a.diagonal_first(m_a)
    pv = pv + page_values(range(quarter, 2 * quarter))
    m_b = tile_b.diagonal_first(m_b)
    pv = pv + page_values(range(2 * quarter, 3 * quarter))
    tile_a.diagonal_second(m_a, late_a)
    pv = pv + page_values(range(3 * quarter, PAGES_PER_STEP))
    tile_b.diagonal_second(m_b, late_b)
    accs_ref[...] = alpha_s * accs_ref[...] + pv

    lam = _lambda(lq1_ref, lk1_ref, lq2_ref, lk2_ref)
    for tile in (tile_a, tile_b):
        rows = pl.ds(pl.multiple_of(tile.qi * tq, tq), tq)
        o_ref[0, rows, :] = tile.finish(lam, subln_col_ref[...]).astype(o_ref.dtype)

    @pl.when(group == groups - 1)
    def _():
        ons = accs_ref[...] / ls_ref[...]
        os_ref[0] = _sub_norm(ons[:ATT_HEADS] - lam * ons[ATT_HEADS:], subln_row_ref[...])


def _attention(qt, kb, vt, page_table, qs, kn, vn, cache_kt, cache_v, lq1, lk1, lq2, lk2, subln, tq):
    b, _, l = qt.shape
    m = qs.shape[0]
    tk = FLASH_TK
    nq = l // tq
    nt = nq // 2
    groups = N_PAGES // PAGES_PER_STEP
    assert tq == 2 * tk
    assert nq % 2 == 0 and b * ATT_HEADS * nt == m * groups
    spread = (jnp.arange(PAGE_SIZE)[:, None] == jnp.arange(PAGE_SIZE * ATT_HEADS)[None, :] // ATT_HEADS).astype(BF16)

    def step(i, h, t):
        return (i * ATT_HEADS + h) * nt + t

    small = pl.BlockSpec((1, HEAD_DIM), lambda i, h, t, pt: (0, 0))
    row = pl.BlockSpec((1, 1, D_MODEL), lambda i, h, t, pt: (step(i, h, t) // groups, 0, 0))
    heads = pl.BlockSpec((1, ATT_HEADS, V_DIM), lambda i, h, t, pt: (step(i, h, t) // groups, 0, 0))

    def page(r):
        def index(i, h, t, pt):
            s = step(i, h, t)
            return (pt[(s // groups) * N_PAGES + (s % groups) * PAGES_PER_STEP + r], 0, 0)
        return pl.BlockSpec((1, K_DIM, PAGE_SIZE), index)

    pages = [page(r) for r in range(PAGES_PER_STEP)]
    tile_scratch = [pltpu.VMEM((VT_ROWS, 2 * tq), F32), pltpu.VMEM((tk, 2 * tq), F32), pltpu.VMEM((tk, 2 * tq), F32)]
    grid_spec = pltpu.PrefetchScalarGridSpec(
        num_scalar_prefetch=1,
        grid=(b, ATT_HEADS, nt),
        in_specs=[pl.BlockSpec((1, LANES, tq), lambda i, h, t, pt: (i, h, t)),
                  pl.BlockSpec((1, LANES, tq), lambda i, h, t, pt: (i, h, nq - 1 - t)),
                  pl.BlockSpec((1, l, LANES), lambda i, h, t, pt: (i, 0, h)),
                  pl.BlockSpec((1, l // tk, 1, VT_ROWS, tk), lambda i, h, t, pt: (i, 0, h, 0, 0)),
                  small, small, small, small,
                  pl.BlockSpec((V_DIM, 1), lambda i, h, t, pt: (0, 0)),
                  pl.BlockSpec((1, V_DIM), lambda i, h, t, pt: (0, 0)),
                  pl.BlockSpec(spread.shape, lambda i, h, t, pt: (0, 0)),
                  row, row, heads] + pages + pages,
        out_specs=[pl.BlockSpec((1, l, LANES), lambda i, h, t, pt: (i, 0, h)), heads],
        scratch_shapes=tile_scratch + tile_scratch + [
            pltpu.VMEM((N_MAPS, K_DIM), BF16),
            pltpu.VMEM((N_MAPS, 1), F32),
            pltpu.VMEM((N_MAPS, 1), F32),
            pltpu.VMEM((N_MAPS, V_DIM), F32),
            pltpu.VMEM((N_MAPS, PAGES_PER_STEP * PAGE_SIZE), F32),
            pltpu.VMEM((PAGES_PER_STEP * N_MAPS, PAGE_SIZE * ATT_HEADS), BF16)],
    )
    o, o_s = pl.pallas_call(
        functools.partial(_attention_kernel, tq=tq, tk=tk, groups=groups),
        grid_spec=grid_spec,
        out_shape=[jax.ShapeDtypeStruct((b, l, VAL_WIDTH), BF16),
                   jax.ShapeDtypeStruct((m, ATT_HEADS, V_DIM), F32)],
        compiler_params=_params("arbitrary", "arbitrary", "arbitrary"),
        name="attention",
    )(page_table.reshape(-1), qt, qt, kb, vt, lq1, lk1, lq2, lk2, subln.reshape(V_DIM, 1), subln, spread,
      qs.reshape(m, 1, K_DIM), kn.reshape(m, 1, K_DIM), vn.reshape(m, ATT_HEADS, V_DIM),
      *([cache_kt] * PAGES_PER_STEP), *([cache_v] * PAGES_PER_STEP))
    return o, o_s.reshape(m, VAL_WIDTH)


def _attn_out_kernel(o_ref, gate_ref, h_ref, wout_ref, gf_ref, y_ref):
    og = (o_ref[...] * _silu(gate_ref[...])).astype(BF16)
    y_ref[...] = _rms(h_ref[...] + _dot(og, wout_ref[...])) * gf_ref[...]


def _attn_out(o, gate, h, wout, gf, tm):
    m = o.shape[0]
    row = pl.BlockSpec((tm, D_MODEL), lambda i: (i, 0))
    return pl.pallas_call(
        _attn_out_kernel,
        grid=(m // tm,),
        in_specs=[row, row, row, _full(wout.shape), _full((1, D_MODEL))],
        out_specs=row,
        out_shape=jax.ShapeDtypeStruct((m, D_MODEL), F32),
        compiler_params=_params("arbitrary"),
        name="attn_out",
    )(o, gate, h, wout, gf)


def _pad_lanes(v):
    return jnp.pad(v.reshape(1, -1), ((0, 0), (0, LANES - v.shape[-1])))


def kernel(x_prompt, x_sample, cache_k, cache_v, page_table, state_conv, state_ssm, norm_a, w_in_a, conv_w, conv_b, dt_bias, a_log, d_skip, gnorm_a, w_out_a, norm_kv, w_kv, norm_b, w_in_b, lambda_q1, lambda_k1, lambda_q2, lambda_k2, subln_b, w_out_b, norm_f):
    bp, lp, _ = x_prompt.shape
    bs = x_sample.shape[0]
    mp = bp * lp

    w_a = w_in_a[0].astype(BF16)
    wz, wx = w_a[:, :D_INNER], w_a[:, D_INNER:D_INNER + CONV_DIM]
    wdt = jnp.pad(w_a[:, D_INNER + CONV_DIM:], ((0, 0), (0, LANES - SSM_HEADS)))
    wout_a = w_out_a[0].astype(BF16)
    wkv = w_kv.astype(BF16)
    wb = w_in_b[0].astype(BF16)
    wout_b = w_out_b[0].astype(BF16)
    g_a = norm_a[0].reshape(1, D_MODEL)
    convw, convb = conv_w[0], conv_b[0].reshape(1, CONV_DIM)
    dtb, alog = _pad_lanes(dt_bias[0]), _pad_lanes(a_log[0])
    dskip = jnp.repeat(d_skip[0], SSM_HEAD_DIM).reshape(1, D_INNER)
    gnorm = gnorm_a[0].reshape(1, D_INNER)
    g_kv, g_b, g_f = norm_kv.reshape(1, D_MODEL), norm_b[0].reshape(1, D_MODEL), norm_f.reshape(1, D_MODEL)
    lq1, lk1, lq2, lk2 = (t[0].reshape(1, HEAD_DIM) for t in (lambda_q1, lambda_k1, lambda_q2, lambda_k2))
    subln = subln_b[0].reshape(1, V_DIM)
    head_of_lane = jnp.arange(D_INNER) // SSM_HEAD_DIM
    hexp = (jnp.arange(LANES)[:, None] == head_of_lane[None, :]).astype(F32)
    gexp = (jnp.arange(N_GROUPS)[:, None] == (head_of_lane // (SSM_HEADS // N_GROUPS))[None, :]).astype(F32)

    xp = x_prompt.reshape(mp, D_MODEL)
    h1, conv_p, ssm_p = _ssd_prompt(xp, bp, g_a, wz, wx, wdt, convw, convb, dtb, alog, dskip, gnorm, wout_a)
    tables_p = _rope_tables(jnp.arange(lp, dtype=F32))
    kt_p, v_p, kb_p, vt_p, qt_p, gate_p = _kvq_flash(h1, g_kv, g_b, wkv, wb, *tables_p, b=bp, l=lp)
    k_p = jnp.transpose(kt_p.reshape(bp, ATT_HEADS, 2, HEAD_DIM, lp), (0, 4, 1, 2, 3))

    n_pool = cache_k.shape[0]
    cache_kt = jnp.transpose(cache_k, (0, 2, 3, 4, 1)).reshape(n_pool, K_DIM, PAGE_SIZE)
    cache_vr = cache_v.reshape(n_pool, PAGE_SIZE * ATT_HEADS, V_DIM)
    sconv = jnp.transpose(state_conv[0], (1, 0, 2))
    xs_in = x_sample.reshape(bs, D_MODEL)
    z_s, xbc_s, dt_s = _inproj_a(xs_in, g_a, wz, wx, wdt, tm=bs)
    conv_s, xdt, dec, bm, cm, ydiag, xs_act = _ssd_step_pre(xbc_s, dt_s, sconv, convw, convb, dtb, alog, hexp, gexp)
    yoff, ssm_s = _ssd_step_state(xdt, dec, bm, cm, state_ssm[0])
    h1_s = _ssd_step_out(ydiag, yoff, dec, xs_act, z_s, xs_in, dskip, gnorm, wout_a)
    tables_s = _rope_tables(jnp.full((bs,), float(PAST_LEN), F32))
    k_s, v_s, q_s, gate_s = _kvq(h1_s, g_kv, g_b, wkv, wb, *tables_s, tm=bs)

    o_p, o_s = _attention(qt_p, kb_p.reshape(bp, lp, K_DIM), vt_p, page_table, q_s, k_s, v_s, cache_kt, cache_vr,
                          lq1, lk1, lq2, lk2, subln, tq=512)
    y_p = _attn_out(o_p.reshape(mp, VAL_WIDTH), gate_p, h1, wout_b, g_f, tm=256)
    y_s = _attn_out(o_s, gate_s, h1_s, wout_b, g_f, tm=bs)

    return (y_p.reshape(bp, lp, D_MODEL), y_s.reshape(bs, 1, D_MODEL),
            k_p, v_p.reshape(bp, lp, ATT_HEADS, V_DIM),
            conv_p[None], ssm_p[None],
            k_s.reshape(bs, 1, ATT_HEADS, 2, HEAD_DIM), v_s.reshape(bs, 1, ATT_HEADS, V_DIM),
            jnp.transpose(conv_s, (1, 0, 2))[None], ssm_s[None])
```

```python
import functools
import math

import jax
import jax.numpy as jnp
from jax import lax
from jax.experimental import pallas as pl
from jax.experimental.pallas import tpu as pltpu

F32 = jnp.float32
BF16 = jnp.bfloat16
HIGHEST = lax.Precision.HIGHEST

D_MODEL = 1024
PAST_LEN = 8192
PAGE_SIZE = 128
N_PAGES = PAST_LEN // PAGE_SIZE
D_INNER = 2048
SSM_HEAD_DIM = 64
SSM_HEADS = 32
N_GROUPS = 4
D_STATE = 128
D_CONV = 4
CONV_DIM = D_INNER + 2 * N_GROUPS * D_STATE
CHUNK = 128
ATT_HEADS = 8
HEAD_DIM = 64
V_DIM = 128
K_DIM = 1024
VAL_WIDTH = 1024
ROT_DIM = 16
ROPE_THETA = 500000.0
EPS = 1e-6
NEG_INF = -1e30
LAMBDA_INIT = 0.8 - 0.6 * math.exp(-0.3 * 1)

FLASH_TK = 256
VT_ROWS = V_DIM + 16
LOG2_E = math.log2(math.e)
Q_SCALE_LOG2 = HEAD_DIM ** -0.5 * LOG2_E

LANES = 128
VMEM_LIMIT = 56 * 1024 * 1024

NT_DIMS = (((1,), (1,)), ((), ()))


def _params(*sem):
    return pltpu.CompilerParams(dimension_semantics=sem, vmem_limit_bytes=VMEM_LIMIT)


def _silu(x):
    return x * (1.0 / (1.0 + jnp.exp(-x)))


def _softplus(x):
    return jnp.maximum(x, 0.0) + jnp.log1p(jnp.exp(-jnp.abs(x)))


def _rms(x):
    return x * lax.rsqrt(jnp.mean(x * x, axis=-1, keepdims=True) + EPS)


def _dot(a, b):
    return jnp.dot(a, b, preferred_element_type=F32)


def _full(shape):
    return pl.BlockSpec(shape, lambda *_: (0,) * len(shape))


def _inproj_a_kernel(x_ref, g_ref, wz_ref, wx_ref, wdt_ref, z_ref, xbc_ref, dt_ref):
    xn = (_rms(x_ref[...]) * g_ref[...]).astype(BF16)
    z_ref[...] = _dot(xn, wz_ref[...])
    xbc_ref[...] = _dot(xn, wx_ref[...])
    dt_ref[...] = _dot(xn, wdt_ref[...])


def _inproj_a(x, g, wz, wx, wdt, tm):
    m = x.shape[0]
    row = lambda n: pl.BlockSpec((tm, n), lambda i: (i, 0))
    return pl.pallas_call(
        _inproj_a_kernel,
        grid=(m // tm,),
        in_specs=[row(D_MODEL), _full((1, D_MODEL)), _full(wz.shape), _full(wx.shape), _full(wdt.shape)],
        out_specs=[row(D_INNER), row(CONV_DIM), row(LANES)],
        out_shape=[jax.ShapeDtypeStruct((m, D_INNER), F32),
                   jax.ShapeDtypeStruct((m, CONV_DIM), F32),
                   jax.ShapeDtypeStruct((m, LANES), F32)],
        compiler_params=_params("arbitrary"),
        name="inproj_a",
    )(x, g, wz, wx, wdt)


def _gate_norm_outproj(y, gate, gnorm, wout, xres):
    yg = y * gate
    gw = D_INNER // N_GROUPS
    parts = [_rms(yg[:, g * gw:(g + 1) * gw]) for g in range(N_GROUPS)]
    yn = (jnp.concatenate(parts, axis=1) * gnorm).astype(BF16)
    return xres + _dot(yn, wout)


def _project_chunk(x, g_ref, wz_ref, wx_ref, wdt_ref, proj):
    z_ref, xbc_ref, dt_ref = proj
    xn = (_rms(x) * g_ref[...]).astype(BF16)
    z_ref[...] = _dot(xn, wz_ref[...])
    xbc_ref[...] = _dot(xn, wx_ref[...])
    dt_ref[...] = _dot(xn, wdt_ref[...])


PROJ_COLS = 512


def _scan_chunk(proj, xres, convw_ref, convb_ref, dtb_ref, alog_ref, dskip_ref, gnorm_ref, wout_ref,
                xpad_ref, st_ref, work, x_ahead, g_ref, wz_ref, wx_ref, wdt_ref, proj_ahead):
    z_ref, xbc_ref, dt_ref = proj
    za_ref, xbca_ref, dta_ref = proj_ahead
    act_ref, gate_ref, y_ref = work
    q = CHUNK
    xn = (_rms(x_ahead) * g_ref[...]).astype(BF16)
    sub8 = lax.broadcasted_iota(jnp.int32, (q // 8, 8, PROJ_COLS), 1)

    for blk in range(CONV_DIM // PROJ_COLS):
        cols = slice(blk * PROJ_COLS, (blk + 1) * PROJ_COLS)
        xbca_ref[:, cols] = _dot(xn, wx_ref[:, cols])
        xbc = xbc_ref[:, cols]
        xpad_ref[8:8 + q, cols] = xbc
        conv = (convb_ref[:, cols] + xbc * convw_ref[3:4, cols]).reshape(q // 8, 8, PROJ_COLS)
        groups = xpad_ref[:, cols].reshape(q // 8 + 1, 8, PROJ_COLS)
        for k in range(1, D_CONV):
            rot = pltpu.roll(groups, k, 1)
            back_k = jnp.where(sub8 < k, rot[:q // 8], rot[1:])
            conv = conv + back_k * convw_ref[D_CONV - 1 - k:D_CONV - k, cols]
        act_ref[:, cols] = _silu(conv).reshape(q, PROJ_COLS)
        xpad_ref[0:8, cols] = xpad_ref[q:q + 8, cols]
    for blk in range(D_INNER // PROJ_COLS):
        cols = slice(blk * PROJ_COLS, (blk + 1) * PROJ_COLS)
        za_ref[:, cols] = _dot(xn, wz_ref[:, cols])
        gate_ref[:, cols] = _silu(z_ref[:, cols])
    dta_ref[...] = _dot(xn, wdt_ref[...])

    dt = _softplus(dt_ref[...] + dtb_ref[...])
    a = -jnp.exp(alog_ref[...])
    row_i = lax.broadcasted_iota(jnp.int32, (q, q), 0)
    col_i = lax.broadcasted_iota(jnp.int32, (q, q), 1)
    causal = row_i >= col_i
    a_cs = jnp.dot(causal.astype(F32), dt * a, precision=HIGHEST, preferred_element_type=F32)
    a_cs_t = a_cs.T
    dt_t = dt.T
    w_t = dt_t * jnp.exp(a_cs_t[:, q - 1:q] - a_cs_t)
    e_cs = jnp.exp(a_cs)
    first_half = col_i < SSM_HEAD_DIM

    for g in range(N_GROUPS):
        bm = act_ref[:, D_INNER + g * D_STATE:D_INNER + (g + 1) * D_STATE]
        cm = act_ref[:, D_INNER + (N_GROUPS + g) * D_STATE:D_INNER + (N_GROUPS + g + 1) * D_STATE]
        bm_b = bm.astype(BF16)
        cm_b = cm.astype(BF16)
        cb = lax.dot_general(cm_b, bm_b, NT_DIMS, preferred_element_type=F32)
        bm_t = bm.T
        for jj in range(SSM_HEADS // N_GROUPS // 2):
            j = g * (SSM_HEADS // N_GROUPS // 2) + jj
            xs = act_ref[:, j * LANES:(j + 1) * LANES]
            xs_b = xs.astype(BF16)
            zero = jnp.zeros_like(xs_b)
            xbd = jnp.concatenate([jnp.where(first_half, xs_b, zero),
                                   jnp.where(first_half, zero, xs_b)], axis=0)
            m_parts, w_parts = [], []
            for h in (2 * j, 2 * j + 1):
                seg = a_cs[:, h:h + 1] - a_cs_t[h:h + 1, :]
                decay = jnp.exp(jnp.where(causal, seg, -jnp.inf))
                m_parts.append((cb * decay * dt_t[h:h + 1, :]).astype(BF16))
                w_parts.append((bm_t * w_t[h:h + 1, :]).astype(BF16))
            y_diag = _dot(jnp.concatenate(m_parts, axis=1), xbd)
            upd_t = _dot(jnp.concatenate(w_parts, axis=1), xbd)
            st = st_ref[j]
            e_sel = jnp.where(first_half, e_cs[:, 2 * j:2 * j + 1], e_cs[:, 2 * j + 1:2 * j + 2])
            y_off = _dot(cm_b, st.astype(BF16)) * e_sel
            st_ref[j] = st * e_sel[q - 1:q, :] + upd_t
            y_ref[:, j * LANES:(j + 1) * LANES] = y_diag + y_off + dskip_ref[:, j * LANES:(j + 1) * LANES] * xs

    return _gate_norm_outproj(y_ref[...], gate_ref[...], gnorm_ref[...], wout_ref[...], xres)


def _ssd_prompt_kernel(x2_ref, xnext_ref, g_ref, wz_ref, wx_ref, wdt_ref, convw_ref, convb_ref, dtb_ref, alog_ref,
                       dskip_ref, gnorm_ref, wout_ref,
                       h_ref, conv_out_ref, ssm_out_ref,
                       z0_ref, xbc0_ref, dt0_ref, z1_ref, xbc1_ref, dt1_ref,
                       act0_ref, gate0_ref, y0_ref, act1_ref, gate1_ref, y1_ref, xpad_ref, st_ref,
                       *, steps_per_seq):
    i = pl.program_id(0)
    q = CHUNK
    proj0, proj1 = (z0_ref, xbc0_ref, dt0_ref), (z1_ref, xbc1_ref, dt1_ref)
    weights = (g_ref, wz_ref, wx_ref, wdt_ref)
    consts = (convw_ref, convb_ref, dtb_ref, alog_ref, dskip_ref, gnorm_ref, wout_ref)

    @pl.when(i % steps_per_seq == 0)
    def _():
        xpad_ref[0:8, :] = jnp.zeros((8, CONV_DIM), F32)
        st_ref[...] = jnp.zeros_like(st_ref)

    @pl.when(i == 0)
    def _():
        _project_chunk(x2_ref[0:q, :], *weights, proj0)

    h_ref[0:q, :] = _scan_chunk(proj0, x2_ref[0:q, :], *consts, xpad_ref, st_ref, (act0_ref, gate0_ref, y0_ref),
                                x2_ref[q:2 * q, :], *weights, proj1)
    h_ref[q:2 * q, :] = _scan_chunk(proj1, x2_ref[q:2 * q, :], *consts, xpad_ref, st_ref,
                                    (act1_ref, gate1_ref, y1_ref), xnext_ref[...], *weights, proj0)

    @pl.when(i % steps_per_seq == steps_per_seq - 1)
    def _():
        conv_out_ref[0] = xpad_ref[5:8, :]
        for j in range(SSM_HEADS // 2):
            ssm_out_ref[0, 2 * j:2 * j + 2] = st_ref[j].T.reshape(2, SSM_HEAD_DIM, D_STATE)


def _ssd_prompt(x, b, g, wz, wx, wdt, convw, convb, dtb, alog, dskip, gnorm, wout):
    m = x.shape[0]
    steps = m // (2 * CHUNK)
    steps_per_seq = steps // b
    n_chunks = m // CHUNK
    vm = lambda *shape: pltpu.VMEM(shape, F32)
    per_seq = lambda *blk: pl.BlockSpec((1,) + blk, lambda i: (i // steps_per_seq,) + (0,) * len(blk))
    return pl.pallas_call(
        functools.partial(_ssd_prompt_kernel, steps_per_seq=steps_per_seq),
        grid=(steps,),
        in_specs=[pl.BlockSpec((2 * CHUNK, D_MODEL), lambda i: (i, 0)),
                  pl.BlockSpec((CHUNK, D_MODEL), lambda i: (jnp.minimum(2 * i + 2, n_chunks - 1), 0)),
                  _full(g.shape), _full(wz.shape), _full(wx.shape), _full(wdt.shape),
                  _full(convw.shape), _full(convb.shape), _full(dtb.shape), _full(alog.shape),
                  _full(dskip.shape), _full(gnorm.shape), _full(wout.shape)],
        out_specs=[pl.BlockSpec((2 * CHUNK, D_MODEL), lambda i: (i, 0)),
                   per_seq(D_CONV - 1, CONV_DIM),
                   per_seq(SSM_HEADS, SSM_HEAD_DIM, D_STATE)],
        out_shape=[jax.ShapeDtypeStruct((m, D_MODEL), F32),
                   jax.ShapeDtypeStruct((b, D_CONV - 1, CONV_DIM), F32),
                   jax.ShapeDtypeStruct((b, SSM_HEADS, SSM_HEAD_DIM, D_STATE), F32)],
        scratch_shapes=[vm(CHUNK, D_INNER), vm(CHUNK, CONV_DIM), vm(CHUNK, LANES),
                        vm(CHUNK, D_INNER), vm(CHUNK, CONV_DIM), vm(CHUNK, LANES),
                        vm(CHUNK, CONV_DIM), vm(CHUNK, D_INNER), vm(CHUNK, D_INNER),
                        vm(CHUNK, CONV_DIM), vm(CHUNK, D_INNER), vm(CHUNK, D_INNER),
                        vm(CHUNK + 8, CONV_DIM),
                        vm(SSM_HEADS // 2, D_STATE, LANES)],
        compiler_params=_params("arbitrary"),
        name="ssd_prompt",
    )(x, x, g, wz, wx, wdt, convw, convb, dtb, alog, dskip, gnorm, wout)


def _ssd_step_pre_kernel(xbc_ref, dt_ref, sconv_ref, convw_ref, convb_ref, dtb_ref, alog_ref,
                         hexp_ref, gexp_ref,
                         conv_out_ref, xdt_ref, dec_ref, bm_ref, cm_ref, ydiag_ref, xs_ref):
    xbc = xbc_ref[...]
    conv = convb_ref[...] + xbc * convw_ref[3:4, :]
    for k in range(D_CONV - 1):
        conv = conv + sconv_ref[k] * convw_ref[k:k + 1, :]
    conv_out_ref[0] = sconv_ref[1]
    conv_out_ref[1] = sconv_ref[2]
    conv_out_ref[2] = xbc
    act = _silu(conv)
    xs = act[:, :D_INNER]
    bm = act[:, D_INNER:D_INNER + N_GROUPS * D_STATE]
    cm = act[:, D_INNER + N_GROUPS * D_STATE:]
    dt = _softplus(dt_ref[...] + dtb_ref[...])
    dec = jnp.exp(dt * -jnp.exp(alog_ref[...]))
    hexp = hexp_ref[...]
    dt_x = jnp.dot(dt, hexp, precision=HIGHEST, preferred_element_type=F32)
    dec_x = jnp.dot(dec, hexp, precision=HIGHEST, preferred_element_type=F32)
    prod = bm * cm
    cb = jnp.concatenate([jnp.sum(prod[:, g * D_STATE:(g + 1) * D_STATE], axis=1, keepdims=True)
                          for g in range(N_GROUPS)], axis=1)
    cb_x = jnp.dot(cb, gexp_ref[...], precision=HIGHEST, preferred_element_type=F32)
    xdt = xs * dt_x
    xdt_ref[...] = xdt
    dec_ref[...] = dec_x
    bm_ref[...] = bm
    cm_ref[...] = cm
    ydiag_ref[...] = cb_x * xdt
    xs_ref[...] = xs


def _ssd_step_pre(xbc, dt, sconv, convw, convb, dtb, alog, hexp, gexp):
    m = xbc.shape[0]
    gs = N_GROUPS * D_STATE
    return pl.pallas_call(
        _ssd_step_pre_kernel,
        out_shape=[jax.ShapeDtypeStruct((D_CONV - 1, m, CONV_DIM), F32),
                   jax.ShapeDtypeStruct((m, D_INNER), F32),
                   jax.ShapeDtypeStruct((m, D_INNER), F32),
                   jax.ShapeDtypeStruct((m, gs), F32),
                   jax.ShapeDtypeStruct((m, gs), F32),
                   jax.ShapeDtypeStruct((m, D_INNER), F32),
                   jax.ShapeDtypeStruct((m, D_INNER), F32)],
        compiler_params=pltpu.CompilerParams(vmem_limit_bytes=VMEM_LIMIT),
        name="ssd_step_pre",
    )(xbc, dt, sconv, convw, convb, dtb, alog, hexp, gexp)


def _ssd_step_state_kernel(xdt_ref, dec_ref, bm_ref, cm_ref, st_ref, yoff_ref, st_out_ref):
    gw = D_INNER // N_GROUPS
    hg = SSM_HEADS // N_GROUPS
    for g in range(N_GROUPS):
        xdt_rows = jnp.broadcast_to(xdt_ref[0, :, g * gw:(g + 1) * gw], (D_STATE, gw)).T
        dec_rows = jnp.broadcast_to(dec_ref[0, :, g * gw:(g + 1) * gw], (D_STATE, gw)).T
        bm = bm_ref[0, :, g * D_STATE:(g + 1) * D_STATE]
        cm8 = jnp.broadcast_to(cm_ref[0, :, g * D_STATE:(g + 1) * D_STATE], (8, D_STATE)).astype(BF16)
        st = st_ref[0, g * hg:(g + 1) * hg].reshape(gw, D_STATE)
        y8 = lax.dot_general(cm8, st.astype(BF16), NT_DIMS, preferred_element_type=F32)
        yoff_ref[0, :, g * gw:(g + 1) * gw] = y8[0:1, :]
        st_out_ref[0, g * hg:(g + 1) * hg] = (st * dec_rows + xdt_rows * bm).reshape(hg, SSM_HEAD_DIM, D_STATE)


def _ssd_step_state(xdt, dec, bm, cm, st):
    m = xdt.shape[0]
    gs = N_GROUPS * D_STATE
    row = lambda n: pl.BlockSpec((1, 1, n), lambda i: (i, 0, 0))
    st_spec = pl.BlockSpec((1, SSM_HEADS, SSM_HEAD_DIM, D_STATE), lambda i: (i, 0, 0, 0))
    yoff, st_out = pl.pallas_call(
        _ssd_step_state_kernel,
        grid=(m,),
        in_specs=[row(D_INNER), row(D_INNER), row(gs), row(gs), st_spec],
        out_specs=[row(D_INNER), st_spec],
        out_shape=[jax.ShapeDtypeStruct((m, 1, D_INNER), F32),
                   jax.ShapeDtypeStruct(st.shape, F32)],
        compiler_params=_params("arbitrary"),
        name="ssd_step_state",
    )(xdt.reshape(m, 1, D_INNER), dec.reshape(m, 1, D_INNER), bm.reshape(m, 1, gs), cm.reshape(m, 1, gs), st)
    return yoff.reshape(m, D_INNER), st_out


def _ssd_step_out_kernel(ydiag_ref, yoff_ref, dec_ref, xs_ref, z_ref, x_ref, dskip_ref, gnorm_ref, wout_ref,
                         h_ref):
    y = ydiag_ref[...] + yoff_ref[...] * dec_ref[...] + dskip_ref[...] * xs_ref[...]
    h_ref[...] = _gate_norm_outproj(y, _silu(z_ref[...]), gnorm_ref[...], wout_ref[...], x_ref[...])


def _ssd_step_out(ydiag, yoff, dec, xs, z, x, dskip, gnorm, wout):
    return pl.pallas_call(
        _ssd_step_out_kernel,
        out_shape=jax.ShapeDtypeStruct(x.shape, F32),
        compiler_params=pltpu.CompilerParams(vmem_limit_bytes=VMEM_LIMIT),
        name="ssd_step_out",
    )(ydiag, yoff, dec, xs, z, x, dskip, gnorm, wout)


def _rope(t, cos, sin_up, sin_dn):
    cols = []
    for c in range(t.shape[1] // LANES):
        tc = t[:, c * LANES:(c + 1) * LANES]
        cols.append(tc * cos + pltpu.roll(tc, ROT_DIM // 2, 1) * sin_up
                    + pltpu.roll(tc, LANES - ROT_DIM // 2, 1) * sin_dn)
    return jnp.concatenate(cols, axis=1)


def _kvq_kernel(h_ref, gkv_ref, gb_ref, wkv_ref, wb_ref, cos_ref, sup_ref, sdn_ref, *out_refs, for_flash):
    xh = _rms(h_ref[...])
    kv = _dot((xh * gkv_ref[...]).astype(BF16), wkv_ref[...])
    qg = _dot((xh * gb_ref[...]).astype(BF16), wb_ref[...])
    cos, sup, sdn = cos_ref[...], sup_ref[...], sdn_ref[...]
    k = _rope(kv[:, :K_DIM], cos, sup, sdn)
    v = kv[:, K_DIM:]
    q = _rope(qg[:, :K_DIM], cos, sup, sdn)
    if for_flash:
        kt_ref, v_ref, kb_ref, vt_ref, qt_ref, gate_ref = out_refs
        kt_ref[0] = k.T
        kb_ref[...] = k.astype(BF16)
        vt = v.T.astype(BF16)
        for j in range(vt_ref.shape[1]):
            for h in range(ATT_HEADS):
                vt_ref[0, j, h, :V_DIM, :] = vt[h * V_DIM:(h + 1) * V_DIM, j * FLASH_TK:(j + 1) * FLASH_TK]
                vt_ref[0, j, h, V_DIM:, :] = jnp.ones((VT_ROWS - V_DIM, FLASH_TK), BF16)
        qt_ref[0] = (q * Q_SCALE_LOG2).T.astype(BF16)
    else:
        k_ref, v_ref, q_ref, gate_ref = out_refs
        k_ref[...] = k
        q_ref[...] = q
    v_ref[...] = v
    gate_ref[...] = qg[:, K_DIM:].astype(gate_ref.dtype)


def _kvq_in_specs(tm, wkv, wb, table_blocks):
    table = pl.BlockSpec((tm, LANES), lambda i: (i % table_blocks, 0))
    return [pl.BlockSpec((tm, D_MODEL), lambda i: (i, 0)), _full((1, D_MODEL)), _full((1, D_MODEL)),
            _full(wkv.shape), _full(wb.shape), table, table, table]


def _kvq(h, gkv, gb, wkv, wb, cos, sup, sdn, tm):
    m = h.shape[0]
    row = pl.BlockSpec((tm, D_MODEL), lambda i: (i, 0))
    out = jax.ShapeDtypeStruct((m, D_MODEL), F32)
    return pl.pallas_call(
        functools.partial(_kvq_kernel, for_flash=False),
        grid=(m // tm,),
        in_specs=_kvq_in_specs(tm, wkv, wb, cos.shape[0] // tm),
        out_specs=[row] * 4,
        out_shape=[out] * 4,
        compiler_params=_params("arbitrary"),
        name="kvq_proj",
    )(h, gkv, gb, wkv, wb, cos, sup, sdn)


def _kvq_flash(h, gkv, gb, wkv, wb, cos, sup, sdn, b, l):
    kpt = 2
    tm = kpt * FLASH_TK
    nb = l // tm
    row = pl.BlockSpec((tm, D_MODEL), lambda i: (i, 0))
    feat = pl.BlockSpec((1, D_MODEL, tm), lambda i: (i // nb, 0, i % nb))
    return pl.pallas_call(
        functools.partial(_kvq_kernel, for_flash=True),
        grid=(b * nb,),
        in_specs=_kvq_in_specs(tm, wkv, wb, nb),
        out_specs=[feat, row, row,
                   pl.BlockSpec((1, kpt, ATT_HEADS, VT_ROWS, FLASH_TK), lambda i: (i // nb, i % nb, 0, 0, 0)),
                   feat, row],
        out_shape=[jax.ShapeDtypeStruct((b, K_DIM, l), F32),
                   jax.ShapeDtypeStruct((b * l, VAL_WIDTH), F32),
                   jax.ShapeDtypeStruct((b * l, K_DIM), BF16),
                   jax.ShapeDtypeStruct((b, l // FLASH_TK, ATT_HEADS, VT_ROWS, FLASH_TK), BF16),
                   jax.ShapeDtypeStruct((b, K_DIM, l), BF16),
                   jax.ShapeDtypeStruct((b * l, VAL_WIDTH), BF16)],
        compiler_params=_params("arbitrary"),
        name="kvq_proj_flash",
    )(h, gkv, gb, wkv, wb, cos, sup, sdn)


def _rope_tables(pos):
    inv_freq = ROPE_THETA ** (-jnp.arange(0, ROT_DIM, 2, dtype=F32) / ROT_DIM)
    ang = pos[:, None] * inv_freq[None, :]
    cos, sin = jnp.cos(ang), jnp.sin(ang)
    half = ROT_DIM // 2
    rows = pos.shape[0]
    pad = jnp.zeros((rows, HEAD_DIM - ROT_DIM), F32)
    zero = jnp.zeros((rows, half), F32)
    cos64 = jnp.concatenate([cos, cos, pad + 1.0], axis=1)
    sup64 = jnp.concatenate([zero, sin, pad], axis=1)
    sdn64 = jnp.concatenate([-sin, zero, pad], axis=1)
    return tuple(jnp.tile(t, (1, LANES // HEAD_DIM)) for t in (cos64, sup64, sdn64))


def _lambda(lq1_ref, lk1_ref, lq2_ref, lk2_ref):
    s1 = jnp.sum(lq1_ref[...] * lk1_ref[...], axis=1, keepdims=True)
    s2 = jnp.sum(lq2_ref[...] * lk2_ref[...], axis=1, keepdims=True)
    return jnp.exp(s1) - jnp.exp(s2) + LAMBDA_INIT


def _sub_norm(o, subln):
    return _rms(o) * subln * (1.0 - LAMBDA_INIT)


PAGES_PER_STEP = 16
N_MAPS = 2 * ATT_HEADS


class _FlashTile:
    def __init__(self, qi, qt_ref, k_ref, vt_ref, acc_ref, sa_ref, sb_ref, tq, tk):
        self.qi, self.k_ref, self.vt_ref = qi, k_ref, vt_ref
        self.acc_ref, self.sa_ref, self.sb_ref, self.tq, self.tk = acc_ref, sa_ref, sb_ref, tq, tk
        qt = qt_ref[0]
        feat = lax.broadcasted_iota(jnp.int32, (LANES, tq), 0)
        zero = jnp.zeros_like(qt)
        self.qqt = jnp.concatenate([jnp.where(feat < HEAD_DIM, qt, zero), jnp.where(feat < HEAD_DIM, zero, qt)],
                                   axis=1)

    def logits(self, kb, lanes=slice(None)):
        start = pl.multiple_of(kb * self.tk, self.tk)
        return _dot(self.k_ref[0, pl.ds(start, self.tk), :], self.qqt[:, lanes])

    def softmax_pv(self, st, m, kb, lanes=slice(None)):
        m_new = jnp.maximum(m, jnp.max(st, axis=0, keepdims=True))
        alpha = jnp.exp2(m - m_new)
        p = jnp.exp2(st - m_new).astype(BF16)
        self.acc_ref[:, lanes] = self.acc_ref[:, lanes] * alpha + _dot(self.vt_ref[0, kb, 0], p)
        return m_new

    def start(self):
        self.acc_ref[...] = jnp.zeros_like(self.acc_ref)
        self.sa_ref[...] = self.logits(0)

    def below_diagonal(self):
        def pair(i, m):
            kb = 2 * i
            self.sb_ref[...] = self.logits(kb + 1)
            m = self.softmax_pv(self.sa_ref[...], m, kb)
            self.sa_ref[...] = self.logits(kb + 2)
            return self.softmax_pv(self.sb_ref[...], m, kb + 1)

        return lax.fori_loop(0, self.qi, pair, jnp.full((1, 2 * self.tq), NEG_INF, F32))

    def late_lanes(self):
        return (slice(self.tk, self.tq), slice(self.tq + self.tk, 2 * self.tq))

    def late_logits(self):
        tk = self.tk
        tri = lax.broadcasted_iota(jnp.int32, (tk, tk), 0) <= lax.broadcasted_iota(jnp.int32, (tk, tk), 1)
        return [jnp.where(tri, self.logits(2 * self.qi + 1, lanes), NEG_INF) for lanes in self.late_lanes()]

    def diagonal_first(self, m):
        n = 2 * self.tq
        key_i = lax.broadcasted_iota(jnp.int32, (self.tk, n), 0)
        qry_i = lax.broadcasted_iota(jnp.int32, (self.tk, n), 1) % self.tq
        return self.softmax_pv(jnp.where(key_i <= qry_i, self.sa_ref[...], NEG_INF), m, 2 * self.qi)

    def diagonal_second(self, m, late_logits):
        for lanes, st in zip(self.late_lanes(), late_logits):
            self.softmax_pv(st, m[:, lanes], 2 * self.qi + 1, lanes)

    def finish(self, lam, subln_col):
        acc_ref, tq = self.acc_ref, self.tq
        on = acc_ref[:V_DIM, :] * (1.0 / acc_ref[V_DIM:V_DIM + 1, :])
        ot = on[:, :tq] - lam * on[:, tq:]
        ot = ot * lax.rsqrt(jnp.mean(ot * ot, axis=0, keepdims=True) + EPS) * subln_col * (1.0 - LAMBDA_INIT)
        return ot.T


def _attention_kernel(pt_ref, qta_ref, qtb_ref, k_ref, vt_ref, lq1_ref, lk1_ref, lq2_ref, lk2_ref, subln_col_ref,
                      subln_row_ref, spread_ref, qs_ref, kn_ref, vn_ref, *rest, tq, tk, groups):
    kt_refs = rest[:PAGES_PER_STEP]
    v_refs = rest[PAGES_PER_STEP:2 * PAGES_PER_STEP]
    (o_ref, os_ref, acca_ref, saa_ref, sba_ref, accb_ref, sab_ref, sbb_ref,
     qrow_ref, ms_ref, ls_ref, accs_ref, ss_ref, ps_ref) = rest[2 * PAGES_PER_STEP:]
    t = pl.program_id(2)
    nt = pl.num_programs(2)
    step = (pl.program_id(0) * pl.num_programs(1) + pl.program_id(1)) * nt + t
    group = step % groups
    quarter = PAGES_PER_STEP // 4

    @pl.when(group == 0)
    def _():
        map_i = lax.broadcasted_iota(jnp.int32, (N_MAPS, K_DIM), 0)
        seg = 2 * (map_i % ATT_HEADS) + map_i // ATT_HEADS
        lane_seg = lax.broadcasted_iota(jnp.int32, (N_MAPS, K_DIM), 1) // HEAD_DIM
        qs = qs_ref[0] * (HEAD_DIM ** -0.5)
        qrow = jnp.where(lane_seg == seg, jnp.broadcast_to(qs, (N_MAPS, K_DIM)), 0.0)
        qrow_ref[...] = qrow.astype(BF16)
        ms_ref[...] = jnp.sum(qrow * kn_ref[0], axis=1, keepdims=True)
        ls_ref[...] = jnp.ones_like(ls_ref)
        accs_ref[...] = jnp.concatenate([vn_ref[0], vn_ref[0]], axis=0)

    tile_a = _FlashTile(t, qta_ref, k_ref, vt_ref, acca_ref, saa_ref, sba_ref, tq, tk)
    tile_b = _FlashTile(2 * nt - 1 - t, qtb_ref, k_ref, vt_ref, accb_ref, sab_ref, sbb_ref, tq, tk)

    qrow = qrow_ref[...]

    def page_scores(pages):
        for i in pages[::2]:
            ss_ref[:, i * PAGE_SIZE:(i + 2) * PAGE_SIZE] = _dot(
                qrow, jnp.concatenate([kt_refs[i][0].astype(BF16), kt_refs[i + 1][0].astype(BF16)], axis=1))

    page_scores(range(2 * quarter))
    tile_a.start()
    page_scores(range(2 * quarter, PAGES_PER_STEP))
    tile_b.start()
    m_a = tile_a.below_diagonal()
    m_b = tile_b.below_diagonal()

    s = ss_ref[...]
    ms_old = ms_ref[...]
    ms_new = jnp.maximum(ms_old, jnp.max(s, axis=1, keepdims=True))
    alpha_s = jnp.exp(ms_old - ms_new)
    ps = jnp.exp(s - ms_new)
    ls_ref[...] = alpha_s * ls_ref[...] + jnp.sum(ps, axis=1, keepdims=True)
    ms_ref[...] = ms_new
    p_rows = jnp.concatenate([ps[:, i * PAGE_SIZE:(i + 1) * PAGE_SIZE] for i in range(PAGES_PER_STEP)], axis=0)
    p_wide = _dot(p_rows.astype(BF16), spread_ref[...])
    own_head = (lax.broadcasted_iota(jnp.int32, p_wide.shape, 0) % ATT_HEADS
                == lax.broadcasted_iota(jnp.int32, p_wide.shape, 1) % ATT_HEADS)
    ps_ref[...] = jnp.where(own_head, p_wide, 0.0).astype(BF16)

    def page_values(pages):
        pv = jnp.zeros((N_MAPS, V_DIM), F32)
        for i in pages[::2]:
            two = _dot(ps_ref[i * N_MAPS:(i + 2) * N_MAPS, :],
                       jnp.concatenate([v_refs[i][0].astype(BF16), v_refs[i + 1][0].astype(BF16)], axis=1))
            pv = pv + two[:N_MAPS, :V_DIM] + two[N_MAPS:, V_DIM:]
        return pv

    late_a = tile_a.late_logits()
    late_b = tile_b.late_logits()
    pv = page_values(range(quarter))
    m_a = tile_a.diagonal_first(m_a)
    pv = pv + page_values(range(quarter, 2 * quarter))
    m_b = tile_b.diagonal_first(m_b)
    pv = pv + page_values(range(2 * quarter, 3 * quarter))
    tile_a.diagonal_second(m_a, late_a)
    pv = pv + page_values(range(3 * quarter, PAGES_PER_STEP))
    tile_b.diagonal_second(m_b, late_b)
    accs_ref[...] = alpha_s * accs_ref[...] + pv

    lam = _lambda(lq1_ref, lk1_ref, lq2_ref, lk2_ref)
    for tile in (tile_a, tile_b):
        rows = pl.ds(pl.multiple_of(tile.qi * tq, tq), tq)
        o_ref[0, rows, :] = tile.finish(lam, subln_col_ref[...]).astype(o_ref.dtype)

    @pl.when(group == groups - 1)
    def _():
        ons = accs_ref[...] / ls_ref[...]
        os_ref[0] = _sub_norm(ons[:ATT_HEADS] - lam * ons[ATT_HEADS:], subln_row_ref[...])


def _attention(qt, kb, vt, page_table, qs, kn, vn, cache_kt, cache_v, lq1, lk1, lq2, lk2, subln, tq):
    b, _, l = qt.shape
    m = qs.shape[0]
    tk = FLASH_TK
    nq = l // tq
    nt = nq // 2
    groups = N_PAGES // PAGES_PER_STEP
    assert tq == 2 * tk
    assert nq % 2 == 0 and b * ATT_HEADS * nt == m * groups
    spread = (jnp.arange(PAGE_SIZE)[:, None] == jnp.arange(PAGE_SIZE * ATT_HEADS)[None, :] // ATT_HEADS).astype(BF16)

    def step(i, h, t):
        return (i * ATT_HEADS + h) * nt + t

    small = pl.BlockSpec((1, HEAD_DIM), lambda i, h, t, pt: (0, 0))
    row = pl.BlockSpec((1, 1, D_MODEL), lambda i, h, t, pt: (step(i, h, t) // groups, 0, 0))
    heads = pl.BlockSpec((1, ATT_HEADS, V_DIM), lambda i, h, t, pt: (step(i, h, t) // groups, 0, 0))

    def page(r):
        def index(i, h, t, pt):
            s = step(i, h, t)
            return (pt[(s // groups) * N_PAGES + (s % groups) * PAGES_PER_STEP + r], 0, 0)
        return pl.BlockSpec((1, K_DIM, PAGE_SIZE), index)

    pages = [page(r) for r in range(PAGES_PER_STEP)]
    tile_scratch = [pltpu.VMEM((VT_ROWS, 2 * tq), F32), pltpu.VMEM((tk, 2 * tq), F32), pltpu.VMEM((tk, 2 * tq), F32)]
    grid_spec = pltpu.PrefetchScalarGridSpec(
        num_scalar_prefetch=1,
        grid=(b, ATT_HEADS, nt),
        in_specs=[pl.BlockSpec((1, LANES, tq), lambda i, h, t, pt: (i, h, t)),
                  pl.BlockSpec((1, LANES, tq), lambda i, h, t, pt: (i, h, nq - 1 - t)),
                  pl.BlockSpec((1, l, LANES), lambda i, h, t, pt: (i, 0, h)),
                  pl.BlockSpec((1, l // tk, 1, VT_ROWS, tk), lambda i, h, t, pt: (i, 0, h, 0, 0)),
                  small, small, small, small,
                  pl.BlockSpec((V_DIM, 1), lambda i, h, t, pt: (0, 0)),
                  pl.BlockSpec((1, V_DIM), lambda i, h, t, pt: (0, 0)),
                  pl.BlockSpec(spread.shape, lambda i, h, t, pt: (0, 0)),
                  row, row, heads] + pages + pages,
        out_specs=[pl.BlockSpec((1, l, LANES), lambda i, h, t, pt: (i, 0, h)), heads],
        scratch_shapes=tile_scratch + tile_scratch + [
            pltpu.VMEM((N_MAPS, K_DIM), BF16),
            pltpu.VMEM((N_MAPS, 1), F32),
            pltpu.VMEM((N_MAPS, 1), F32),
            pltpu.VMEM((N_MAPS, V_DIM), F32),
            pltpu.VMEM((N_MAPS, PAGES_PER_STEP * PAGE_SIZE), F32),
            pltpu.VMEM((PAGES_PER_STEP * N_MAPS, PAGE_SIZE * ATT_HEADS), BF16)],
    )
    o, o_s = pl.pallas_call(
        functools.partial(_attention_kernel, tq=tq, tk=tk, groups=groups),
        grid_spec=grid_spec,
        out_shape=[jax.ShapeDtypeStruct((b, l, VAL_WIDTH), BF16),
                   jax.ShapeDtypeStruct((m, ATT_HEADS, V_DIM), F32)],
        compiler_params=_params("arbitrary", "arbitrary", "arbitrary"),
        name="attention",
    )(page_table.reshape(-1), qt, qt, kb, vt, lq1, lk1, lq2, lk2, subln.reshape(V_DIM, 1), subln, spread,
      qs.reshape(m, 1, K_DIM), kn.reshape(m, 1, K_DIM), vn.reshape(m, ATT_HEADS, V_DIM),
      *([cache_kt] * PAGES_PER_STEP), *([cache_v] * PAGES_PER_STEP))
    return o, o_s.reshape(m, VAL_WIDTH)


def _attn_out_kernel(o_ref, gate_ref, h_ref, wout_ref, gf_ref, y_ref):
    og = (o_ref[...] * _silu(gate_ref[...].astype(F32))).astype(BF16)
    y_ref[...] = _rms(h_ref[...] + _dot(og, wout_ref[...])) * gf_ref[...]


def _attn_out(o, gate, h, wout, gf, tm):
    m = o.shape[0]
    row = pl.BlockSpec((tm, D_MODEL), lambda i: (i, 0))
    return pl.pallas_call(
        _attn_out_kernel,
        grid=(m // tm,),
        in_specs=[row, row, row, _full(wout.shape), _full((1, D_MODEL))],
        out_specs=row,
        out_shape=jax.ShapeDtypeStruct((m, D_MODEL), F32),
        compiler_params=_params("arbitrary"),
        name="attn_out",
    )(o, gate, h, wout, gf)


def _pad_lanes(v):
    return jnp.pad(v.reshape(1, -1), ((0, 0), (0, LANES - v.shape[-1])))


def kernel(x_prompt, x_sample, cache_k, cache_v, page_table, state_conv, state_ssm, norm_a, w_in_a, conv_w, conv_b, dt_bias, a_log, d_skip, gnorm_a, w_out_a, norm_kv, w_kv, norm_b, w_in_b, lambda_q1, lambda_k1, lambda_q2, lambda_k2, subln_b, w_out_b, norm_f):
    bp, lp, _ = x_prompt.shape
    bs = x_sample.shape[0]
    mp = bp * lp

    w_a = w_in_a[0].astype(BF16)
    wz, wx = w_a[:, :D_INNER], w_a[:, D_INNER:D_INNER + CONV_DIM]
    wdt = jnp.pad(w_a[:, D_INNER + CONV_DIM:], ((0, 0), (0, LANES - SSM_HEADS)))
    wout_a = w_out_a[0].astype(BF16)
    wkv = w_kv.astype(BF16)
    wb = w_in_b[0].astype(BF16)
    wout_b = w_out_b[0].astype(BF16)
    g_a = norm_a[0].reshape(1, D_MODEL)
    convw, convb = conv_w[0], conv_b[0].reshape(1, CONV_DIM)
    dtb, alog = _pad_lanes(dt_bias[0]), _pad_lanes(a_log[0])
    dskip = jnp.repeat(d_skip[0], SSM_HEAD_DIM).reshape(1, D_INNER)
    gnorm = gnorm_a[0].reshape(1, D_INNER)
    g_kv, g_b, g_f = norm_kv.reshape(1, D_MODEL), norm_b[0].reshape(1, D_MODEL), norm_f.reshape(1, D_MODEL)
    lq1, lk1, lq2, lk2 = (t[0].reshape(1, HEAD_DIM) for t in (lambda_q1, lambda_k1, lambda_q2, lambda_k2))
    subln = subln_b[0].reshape(1, V_DIM)
    head_of_lane = jnp.arange(D_INNER) // SSM_HEAD_DIM
    hexp = (jnp.arange(LANES)[:, None] == head_of_lane[None, :]).astype(F32)
    gexp = (jnp.arange(N_GROUPS)[:, None] == (head_of_lane // (SSM_HEADS // N_GROUPS))[None, :]).astype(F32)

    xp = x_prompt.reshape(mp, D_MODEL)
    h1, conv_p, ssm_p = _ssd_prompt(xp, bp, g_a, wz, wx, wdt, convw, convb, dtb, alog, dskip, gnorm, wout_a)
    tables_p = _rope_tables(jnp.arange(lp, dtype=F32))
    kt_p, v_p, kb_p, vt_p, qt_p, gate_p = _kvq_flash(h1, g_kv, g_b, wkv, wb, *tables_p, b=bp, l=lp)
    k_p = jnp.transpose(kt_p.reshape(bp, ATT_HEADS, 2, HEAD_DIM, lp), (0, 4, 1, 2, 3))

    n_pool = cache_k.shape[0]
    cache_kt = jnp.transpose(cache_k, (0, 2, 3, 4, 1)).reshape(n_pool, K_DIM, PAGE_SIZE)
    cache_vr = cache_v.reshape(n_pool, PAGE_SIZE * ATT_HEADS, V_DIM)
    sconv = jnp.transpose(state_conv[0], (1, 0, 2))
    xs_in = x_sample.reshape(bs, D_MODEL)
    z_s, xbc_s, dt_s = _inproj_a(xs_in, g_a, wz, wx, wdt, tm=bs)
    conv_s, xdt, dec, bm, cm, ydiag, xs_act = _ssd_step_pre(xbc_s, dt_s, sconv, convw, convb, dtb, alog, hexp, gexp)
    yoff, ssm_s = _ssd_step_state(xdt, dec, bm, cm, state_ssm[0])
    h1_s = _ssd_step_out(ydiag, yoff, dec, xs_act, z_s, xs_in, dskip, gnorm, wout_a)
    tables_s = _rope_tables(jnp.full((bs,), float(PAST_LEN), F32))
    k_s, v_s, q_s, gate_s = _kvq(h1_s, g_kv, g_b, wkv, wb, *tables_s, tm=bs)

    o_p, o_s = _attention(qt_p, kb_p.reshape(bp, lp, K_DIM), vt_p, page_table, q_s, k_s, v_s, cache_kt, cache_vr,
                          lq1, lk1, lq2, lk2, subln, tq=512)
    y_p = _attn_out(o_p.reshape(mp, VAL_WIDTH), gate_p, h1, wout_b, g_f, tm=512)
    y_s = _attn_out(o_s, gate_s, h1_s, wout_b, g_f, tm=bs)

    return (y_p.reshape(bp, lp, D_MODEL), y_s.reshape(bs, 1, D_MODEL),
            k_p, v_p.reshape(bp, lp, ATT_HEADS, V_DIM),
            conv_p[None], ssm_p[None],
            k_s.reshape(bs, 1, ATT_HEADS, 2, HEAD_DIM), v_s.reshape(bs, 1, ATT_HEADS, V_DIM),
            jnp.transpose(conv_s, (1, 0, 2))[None], ssm_s[None])
```
